```python
import math
import jax, jax.numpy as jnp
from jax import lax
import numpy as np

D_MODEL = 2048
BATCH = 8
SEQ = 2048
DEPTH = 4

N_MIXERS = 2
HEAD_DIM = 128
ROPE_THETA = 10000.0
NORM_EPS = 1e-6
D_FF = 4 * D_MODEL
Q_BLOCK = 128
NEG = -1e30
BIG = 1e30

NSA_HEADS = D_MODEL // HEAD_DIM
NSA_KV_HEADS = 4
NSA_GROUP = NSA_HEADS // NSA_KV_HEADS
CMP_BLOCK = 32
CMP_STRIDE = 16
CMP_HIDDEN = 2 * HEAD_DIM
SLC_BLOCK = 64
SLC_TOPN = 16
SLC_QCHUNK = 64
WIN = 512
NSA_IN = NSA_HEADS * HEAD_DIM + 6 * NSA_KV_HEADS * HEAD_DIM + 3 * NSA_HEADS

DIFF_HEADS = D_MODEL // (2 * HEAD_DIM)
DIFF_IN = 3 * DIFF_HEADS * 2 * HEAD_DIM

kernel_name = "hybrid_nsa_diffattn_sqrelu_adaln"


def rms_norm(x, g):
    xf = x.astype(jnp.float32)
    y = xf * lax.rsqrt(jnp.mean(xf * xf, axis=-1, keepdims=True) + NORM_EPS)
    return (y * g.astype(jnp.float32)).astype(x.dtype)


def rope(x, pos):
    half = x.shape[-1] // 2
    inv = ROPE_THETA ** (-jnp.arange(half, dtype=jnp.float32) / half)
    ang = pos.astype(jnp.float32)[..., None] * inv
    cos = jnp.cos(ang)[:, :, None, :]
    sin = jnp.sin(ang)[:, :, None, :]
    xf = x.astype(jnp.float32)
    x1, x2 = xf[..., :half], xf[..., half:]
    return jnp.concatenate([x1 * cos - x2 * sin, x2 * cos + x1 * sin], axis=-1).astype(x.dtype)


def _compress(blocks, pos_emb, w1, w2):
    hid = jax.nn.silu(jnp.einsum('bnlgd,ldf->bngf', blocks + pos_emb[None, None, :, None, :], w1))
    return jnp.einsum('bngf,fd->bngd', hid, w2)


def _cmp_to_slc_map(n_cmp, n_slc):
    cs = CMP_STRIDE * np.arange(n_cmp)[:, None]
    ss = SLC_BLOCK * np.arange(n_slc)[None, :]
    ov = np.minimum(cs + CMP_BLOCK, ss + SLC_BLOCK) - np.maximum(cs, ss)
    return (np.clip(ov, 0, None) / CMP_STRIDE).astype(np.float32)


def _window_branch(qg, k, v, scale):
    B, T, G, R, dh = qg.shape
    nb = T // Q_BLOCK
    span = WIN + Q_BLOCK
    kp = jnp.pad(k, ((0, 0), (WIN, 0), (0, 0), (0, 0)))
    vp = jnp.pad(v, ((0, 0), (WIN, 0), (0, 0), (0, 0)))

    def blk(n):
        start = n * Q_BLOCK
        qb = lax.dynamic_slice_in_dim(qg, start, Q_BLOCK, axis=1)
        kb = lax.dynamic_slice_in_dim(kp, start, span, axis=1)
        vb = lax.dynamic_slice_in_dim(vp, start, span, axis=1)
        q_tok = start + jnp.arange(Q_BLOCK)
        k_tok = start - WIN + jnp.arange(span)
        rel = q_tok[:, None] - k_tok[None, :]
        mask = (rel >= 0) & (rel < WIN) & (k_tok[None, :] >= 0)
        s = jnp.einsum('bqgrd,bkgd->bqgrk', qb, kb, preferred_element_type=jnp.float32) * scale
        p = jax.nn.softmax(jnp.where(mask[None, :, None, None, :], s, NEG), axis=-1)
        return jnp.einsum('bqgrk,bkgd->bqgrd', p.astype(vb.dtype), vb)

    o = lax.map(blk, jnp.arange(nb))
    return jnp.moveaxis(o, 0, 1).reshape(B, T, G, R, dh)


def _select_branch(qg, k, v, idx, scale):
    B, T, G, R, dh = qg.shape
    ns = T // SLC_BLOCK
    nsel = idx.shape[-1]
    nc = T // SLC_QCHUNK
    k_blk = k.reshape(B, ns, SLC_BLOCK, G, dh).transpose(0, 3, 1, 2, 4)
    v_blk = v.reshape(B, ns, SLC_BLOCK, G, dh).transpose(0, 3, 1, 2, 4)
    g_ar = jnp.arange(G)[None, :, None]
    qc = qg.reshape(B * nc, SLC_QCHUNK, G, R, dh)
    ic = idx.reshape(B * nc, SLC_QCHUNK, G, nsel)
    bid = jnp.repeat(jnp.arange(B), nc)
    cid = jnp.tile(jnp.arange(nc), B)

    def step(args):
        qb, ib, b, ci = args
        kg = k_blk[b][g_ar, ib]
        vg = v_blk[b][g_ar, ib]
        s = jnp.einsum('qgrd,qgnld->qgrnl', qb, kg, preferred_element_type=jnp.float32) * scale
        key_tok = ib[..., None] * SLC_BLOCK + jnp.arange(SLC_BLOCK)
        q_tok = ci * SLC_QCHUNK + jnp.arange(SLC_QCHUNK)
        mask = key_tok <= q_tok[:, None, None, None]
        p = jax.nn.softmax(jnp.where(mask[:, :, None], s, NEG), axis=(-2, -1))
        return jnp.einsum('qgrnl,qgnld->qgrd', p.astype(vg.dtype), vg)

    o = lax.map(step, (qc, ic, bid, cid))
    return o.reshape(B, T, G, R, dh)


def nsa_mixer(h, pos, w_in, w_out, q_g, k_g, cmp_pos, cmp_w1, cmp_w2):
    B, T, _ = h.shape
    H, G, R, dh = NSA_HEADS, NSA_KV_HEADS, NSA_GROUP, HEAD_DIM
    scale = dh ** -0.5
    sizes = [H * dh] + [G * dh] * 6
    splits = [int(s) for s in np.cumsum(sizes)]
    q, kc, vc, ks, vs, kw, vw, gates = jnp.split(h @ w_in, splits, axis=-1)
    q = rope(rms_norm(q.reshape(B, T, H, dh), q_g), pos)
    qg = q.reshape(B, T, G, R, dh)

    n_cmp = (T - CMP_BLOCK) // CMP_STRIDE + 1
    blk_idx = CMP_STRIDE * np.arange(n_cmp)[:, None] + np.arange(CMP_BLOCK)[None, :]
    kc = kc.reshape(B, T, G, dh)
    vc = vc.reshape(B, T, G, dh)
    k_cmp = _compress(kc[:, blk_idx], cmp_pos[0], cmp_w1[0], cmp_w2[0])
    v_cmp = _compress(vc[:, blk_idx], cmp_pos[1], cmp_w1[1], cmp_w2[1])
    k_cmp = rope(rms_norm(k_cmp, k_g[0]), pos[:, blk_idx[:, -1]])
    cmp_end = jnp.asarray(blk_idx[:, -1])
    cmask = cmp_end[None, :] <= jnp.arange(T)[:, None]
    cmask5 = cmask[None, :, None, None, :]
    s = jnp.einsum('btgrd,bngd->btgrn', qg, k_cmp, preferred_element_type=jnp.float32) * scale
    p_cmp = jnp.where(cmask5, jax.nn.softmax(jnp.where(cmask5, s, NEG), axis=-1), 0.0)
    o_cmp = jnp.einsum('btgrn,bngd->btgrd', p_cmp.astype(v_cmp.dtype), v_cmp)

    n_slc = T // SLC_BLOCK
    n_sel = min(SLC_TOPN, n_slc)
    imp = jnp.einsum('btgn,ns->btgs', p_cmp.sum(axis=3), jnp.asarray(_cmp_to_slc_map(n_cmp, n_slc)))
    t_blk = (jnp.arange(T) // SLC_BLOCK)[:, None]
    s_ids = jnp.arange(n_slc)[None, :]
    valid = (s_ids <= t_blk)[None, :, None, :]
    forced = ((s_ids == 0) | (s_ids == t_blk) | (s_ids == t_blk - 1))[None, :, None, :]
    imp = jnp.where(forced, BIG, jnp.where(valid, imp, NEG))
    _, sel_idx = lax.top_k(imp, n_sel)

    ks = rope(rms_norm(ks.reshape(B, T, G, dh), k_g[1]), pos)
    o_slc = _select_branch(qg, ks, vs.reshape(B, T, G, dh), sel_idx, scale)

    kw = rope(rms_norm(kw.reshape(B, T, G, dh), k_g[2]), pos)
    o_win = _window_branch(qg, kw, vw.reshape(B, T, G, dh), scale)

    g = jax.nn.sigmoid(gates.astype(jnp.float32)).astype(h.dtype).reshape(B, T, 3, G, R, 1)
    o = g[:, :, 0] * o_cmp + g[:, :, 1] * o_slc + g[:, :, 2] * o_win
    return o.reshape(B, T, H * dh) @ w_out


def diff_mixer(h, pos, w_in, w_out, q_g, k_g, lam_p, sub_g, lam_init):
    B, T, _ = h.shape
    H, dh = DIFF_HEADS, HEAD_DIM
    scale = dh ** -0.5
    q, k, v = jnp.split(h @ w_in, 3, axis=-1)
    q = rope(rms_norm(q.reshape(B, T, 2 * H, dh), q_g), pos).reshape(B, T, H, 2, dh)
    k = rope(rms_norm(k.reshape(B, T, 2 * H, dh), k_g), pos).reshape(B, T, H, 2, dh)
    v = v.reshape(B, T, H, 2 * dh)
    lp = lam_p.astype(jnp.float32)
    lam = jnp.exp(jnp.sum(lp[0] * lp[1])) - jnp.exp(jnp.sum(lp[2] * lp[3])) + lam_init
    k_tok = jnp.arange(T)

    def blk(n):
        start = n * Q_BLOCK
        qb = lax.dynamic_slice_in_dim(q, start, Q_BLOCK, axis=1)
        s = jnp.einsum('bqhcd,bkhcd->bhcqk', qb, k, preferred_element_type=jnp.float32) * scale
        mask = k_tok[None, :] <= (start + jnp.arange(Q_BLOCK))[:, None]
        p = jax.nn.softmax(jnp.where(mask, s, NEG), axis=-1)
        a = p[:, :, 0] - lam * p[:, :, 1]
        return jnp.einsum('bhqk,bkhe->bqhe', a.astype(v.dtype), v)

    o = lax.map(blk, jnp.arange(T // Q_BLOCK))
    o = jnp.moveaxis(o, 0, 1).reshape(B, T, H, 2 * dh)
    o = rms_norm(o, sub_g) * (1.0 - lam_init)
    return o.reshape(B, T, H * 2 * dh) @ w_out


def sqrelu_mlp(h, w1, w2):
    return jnp.square(jax.nn.relu(h @ w1)) @ w2


def setup_inputs(seed: int = 0) -> dict:
    key = jax.random.key(seed)
    ks = jax.random.split(key, 24)
    f32 = jnp.float32
    D, L = D_MODEL, DEPTH
    n_nsa = (DEPTH + 1) // 2
    n_diff = DEPTH // 2

    def nrm(k, shape, s):
        return jax.random.normal(k, shape, f32) * s

    x = nrm(ks[0], (BATCH, SEQ, D), 1.0)
    c = nrm(ks[1], (BATCH, D), 1.0)
    positions = (jax.random.randint(ks[2], (BATCH, 1), 0, 1024, dtype=jnp.int32)
                 + jnp.arange(SEQ, dtype=jnp.int32)[None, :])
    return {
        "x": x,
        "c": c,
        "positions": positions,
        "ada_w": nrm(ks[3], (L, D, 6 * D), 0.5 * D ** -0.5),
        "ada_b": nrm(ks[4], (L, 6 * D), 0.01),
        "attn_norm_g": 1.0 + nrm(ks[5], (L, D), 0.02),
        "mlp_norm_g": 1.0 + nrm(ks[6], (L, D), 0.02),
        "mlp_w1": nrm(ks[7], (L, D, D_FF), D ** -0.5),
        "mlp_w2": nrm(ks[8], (L, D_FF, D), D_FF ** -0.5),
        "nsa_w_in": nrm(ks[9], (n_nsa, D, NSA_IN), D ** -0.5),
        "nsa_w_out": nrm(ks[10], (n_nsa, NSA_HEADS * HEAD_DIM, D), (NSA_HEADS * HEAD_DIM) ** -0.5),
        "nsa_q_norm": 1.0 + nrm(ks[11], (n_nsa, HEAD_DIM), 0.02),
        "nsa_k_norm": 1.0 + nrm(ks[12], (n_nsa, 3, HEAD_DIM), 0.02),
        "nsa_cmp_pos": nrm(ks[13], (n_nsa, 2, CMP_BLOCK, HEAD_DIM), 0.1),
        "nsa_cmp_w1": nrm(ks[14], (n_nsa, 2, CMP_BLOCK, HEAD_DIM, CMP_HIDDEN), (CMP_BLOCK * HEAD_DIM) ** -0.5),
        "nsa_cmp_w2": nrm(ks[15], (n_nsa, 2, CMP_HIDDEN, HEAD_DIM), CMP_HIDDEN ** -0.5),
        "diff_w_in": nrm(ks[16], (n_diff, D, DIFF_IN), D ** -0.5),
        "diff_w_out": nrm(ks[17], (n_diff, DIFF_HEADS * 2 * HEAD_DIM, D), (DIFF_HEADS * 2 * HEAD_DIM) ** -0.5),
        "diff_q_norm": 1.0 + nrm(ks[18], (n_diff, HEAD_DIM), 0.02),
        "diff_k_norm": 1.0 + nrm(ks[19], (n_diff, HEAD_DIM), 0.02),
        "diff_lambda": nrm(ks[20], (n_diff, 4, HEAD_DIM), 0.1),
        "diff_sub_norm": 1.0 + nrm(ks[21], (n_diff, 2 * HEAD_DIM), 0.02),
    }


def reference(x, c, positions, ada_w, ada_b, attn_norm_g, mlp_norm_g, mlp_w1, mlp_w2,
              nsa_w_in, nsa_w_out, nsa_q_norm, nsa_k_norm, nsa_cmp_pos, nsa_cmp_w1, nsa_cmp_w2,
              diff_w_in, diff_w_out, diff_q_norm, diff_k_norm, diff_lambda, diff_sub_norm):
    mod_all = jnp.einsum('bd,lde->lbe', jax.nn.silu(c), ada_w) + ada_b[:, None, :]
    for i in range(DEPTH):
        sh1, sc1, g1, sh2, sc2, g2 = jnp.split(mod_all[i][:, None, :], 6, axis=-1)
        h = rms_norm(x, attn_norm_g[i]) * (1.0 + sc1) + sh1
        j = i // N_MIXERS
        if i % N_MIXERS == 0:
            y = nsa_mixer(h, positions, nsa_w_in[j], nsa_w_out[j], nsa_q_norm[j], nsa_k_norm[j],
                          nsa_cmp_pos[j], nsa_cmp_w1[j], nsa_cmp_w2[j])
        else:
            lam_init = 0.8 - 0.6 * math.exp(-0.3 * i)
            y = diff_mixer(h, positions, diff_w_in[j], diff_w_out[j], diff_q_norm[j], diff_k_norm[j],
                           diff_lambda[j], diff_sub_norm[j], lam_init)
        x = x + g1 * y
        h = rms_norm(x, mlp_norm_g[i]) * (1.0 + sc2) + sh2
        x = x + g2 * sqrelu_mlp(h, mlp_w1[i], mlp_w2[i])
    return x
```

```python
import functools
import math

import numpy as np
import jax
import jax.numpy as jnp
from jax import lax
from jax.experimental import pallas as pl
from jax.experimental.pallas import tpu as pltpu

F32 = jnp.float32
BF16 = jnp.bfloat16

HEAD_DIM = 128
ROPE_THETA = 10000.0
NORM_EPS = 1e-6
NEG = -1e30
BIG = 1e30

NSA_KV_HEADS = 4
NSA_GROUP = 4
CMP_BLOCK = 32
CMP_STRIDE = 16
CMP_HIDDEN = 2 * HEAD_DIM
SLC_BLOCK = 64
SLC_TOPN = 16
WIN = 512

LANES = 128
V7X_VMEM_BYTES = 64 * 1024 * 1024
VMEM_CAP = V7X_VMEM_BYTES - 8 * 1024 * 1024

NT_DIMS = (((1,), (1,)), ((), ()))
TN_DIMS = (((0,), (0,)), ((), ()))


def _vmem_limit(nbytes):
    return int(min(VMEM_CAP, max(32 * 1024 * 1024, nbytes)))


def _adaln_body(c_ref, w_ref, b_ref, o_ref):
    c = c_ref[...]
    sc = (c * jax.nn.sigmoid(c)).astype(BF16)
    w = w_ref[0].astype(BF16)
    o_ref[0] = jnp.dot(sc, w, preferred_element_type=F32) + b_ref[0]


def _adaln(c, ada_w, ada_b, tn=1024):
    L, D, E = ada_w.shape
    B = c.shape[0]
    return pl.pallas_call(
        _adaln_body,
        out_shape=jax.ShapeDtypeStruct((L, B, E), F32),
        grid=(L, E // tn),
        in_specs=[
            pl.BlockSpec((B, D), lambda l, j: (0, 0)),
            pl.BlockSpec((1, D, tn), lambda l, j: (l, 0, j)),
            pl.BlockSpec((1, 1, tn), lambda l, j: (l, 0, j)),
        ],
        out_specs=pl.BlockSpec((1, B, tn), lambda l, j: (l, 0, j)),
        compiler_params=pltpu.CompilerParams(
            dimension_semantics=("parallel", "parallel"),
            vmem_limit_bytes=_vmem_limit(2 * D * tn * 4 + D * tn * 2 + (8 << 20))),
        name="adaln_mod",
    )(c, ada_w, ada_b.reshape(L, 1, E))


def _rope_tab_body(pos_ref, inv_ref, sgn_ref, cos_ref, sin_ref):
    ang = pos_ref[...] * inv_ref[...]
    cos_ref[...] = jnp.cos(ang)
    sin_ref[...] = jnp.sin(ang) * sgn_ref[...]


def _rope_tables(positions, tr=1024):
    n = positions.size
    half = HEAD_DIM // 2
    inv = ROPE_THETA ** (-jnp.arange(half, dtype=F32) / half)
    inv = jnp.concatenate([inv, inv]).reshape(1, HEAD_DIM)
    sgn = jnp.concatenate([-jnp.ones((half,), F32), jnp.ones((half,), F32)]).reshape(1, HEAD_DIM)
    posb = jnp.broadcast_to(positions.astype(F32).reshape(n, 1), (n, HEAD_DIM))
    row = pl.BlockSpec((tr, HEAD_DIM), lambda i: (i, 0))
    vec = pl.BlockSpec((1, HEAD_DIM), lambda i: (0, 0))
    return pl.pallas_call(
        _rope_tab_body,
        out_shape=(jax.ShapeDtypeStruct((n, HEAD_DIM), F32),) * 2,
        grid=(n // tr,),
        in_specs=[row, vec, vec],
        out_specs=(row, row),
        compiler_params=pltpu.CompilerParams(dimension_semantics=("parallel",)),
        name="rope_tables",
    )(posb, inv, sgn)


def _norm_rope(y, gain, cosv, sinv):
    ms = jnp.mean(y * y, axis=-1, keepdims=True)
    y = y * lax.rsqrt(ms + NORM_EPS) * gain
    return y * cosv + pltpu.roll(y, HEAD_DIM // 2, axis=1) * sinv


def _mm1_body(x_ref, g_ref, sc_ref, sh_ref, w_ref, *refs, epilogue, n_rope, rc):
    if epilogue == "rope":
        cg_ref, cos_ref, sin_ref, o_ref, h_scr = refs
    else:
        o_ref, h_scr = refs
    j = pl.program_id(1)
    tm = x_ref.shape[0]
    tn = w_ref.shape[1]
    chunks = [slice(r * rc, (r + 1) * rc) for r in range(tm // rc)]

    @pl.when(j == 0)
    def _():
        for rows in chunks:
            x = x_ref[rows, :]
            ms = jnp.mean(x * x, axis=-1, keepdims=True)
            y = x * lax.rsqrt(ms + NORM_EPS) * g_ref[...]
            h_scr[rows, :] = (y * (1.0 + sc_ref[0]) + sh_ref[0]).astype(BF16)

    def compute(rope):
        for rows in chunks:
            acc = jnp.dot(h_scr[rows, :], w_ref[...], preferred_element_type=F32)
            if epilogue == "sqrelu":
                r = jnp.maximum(acc, 0.0)
                o_ref[rows, :] = (r * r).astype(o_ref.dtype)
            elif epilogue == "sigmoid":
                o_ref[rows, :] = jax.nn.sigmoid(acc).astype(o_ref.dtype)
            elif rope:
                cosv = cos_ref[rows, :]
                sinv = sin_ref[rows, :]
                for h in range(tn // HEAD_DIM):
                    cols = slice(h * HEAD_DIM, (h + 1) * HEAD_DIM)
                    y = _norm_rope(acc[:, cols], cg_ref[:, cols], cosv, sinv)
                    o_ref[rows, cols] = y.astype(o_ref.dtype)
            else:
                o_ref[rows, :] = acc.astype(o_ref.dtype)

    if epilogue == "rope":
        pl.when(j < n_rope)(functools.partial(compute, True))
        pl.when(j >= n_rope)(functools.partial(compute, False))
    else:
        compute(False)


def _mm1(x, g, sc, sh, w, *, epilogue, out_dtype, seq, rope=None, n_rope=0,
         tm=1024, tn=1024, rc=256):
    N, D = x.shape
    Nout = w.shape[1]
    tn = min(tn, Nout)
    B = sc.shape[0]
    bpt = seq // tm
    in_specs = [
        pl.BlockSpec((tm, D), lambda i, j: (i, 0)),
        pl.BlockSpec((1, D), lambda i, j: (0, 0)),
        pl.BlockSpec((1, 1, D), lambda i, j: (i // bpt, 0, 0)),
        pl.BlockSpec((1, 1, D), lambda i, j: (i // bpt, 0, 0)),
        pl.BlockSpec((D, tn), lambda i, j: (0, j)),
    ]
    args = [x, g.reshape(1, D), sc.reshape(B, 1, D), sh.reshape(B, 1, D), w]
    if epilogue == "rope":
        colgain, cos_t, sin_t = rope
        in_specs += [
            pl.BlockSpec((1, tn), lambda i, j: (0, j)),
            pl.BlockSpec((tm, HEAD_DIM), lambda i, j: (i, 0)),
            pl.BlockSpec((tm, HEAD_DIM), lambda i, j: (i, 0)),
        ]
        args += [colgain.reshape(1, Nout), cos_t, sin_t]
    osz = jnp.dtype(out_dtype).itemsize
    vmem = (2 * tm * D * 4 + 2 * D * tn * 2 + 2 * tm * tn * osz + tm * D * 2
            + 4 * tm * HEAD_DIM * 4 + 4 * rc * tn * 4 + (6 << 20))
    return pl.pallas_call(
        functools.partial(_mm1_body, epilogue=epilogue, n_rope=n_rope, rc=rc),
        out_shape=jax.ShapeDtypeStruct((N, Nout), out_dtype),
        grid=(N // tm, Nout // tn),
        in_specs=in_specs,
        out_specs=pl.BlockSpec((tm, tn), lambda i, j: (i, j)),
        scratch_shapes=[pltpu.VMEM((tm, D), BF16)],
        compiler_params=pltpu.CompilerParams(
            dimension_semantics=("parallel", "arbitrary"),
            vmem_limit_bytes=_vmem_limit(vmem)),
        name="mm1_" + epilogue,
    )(*args)


def _mm2_body(a_ref, w_ref, x_ref, gate_ref, o_ref, acc_ref, *, nk):
    k = pl.program_id(2)

    @pl.when(k == 0)
    def _():
        acc_ref[...] = jnp.zeros_like(acc_ref)

    acc_ref[...] += jnp.dot(a_ref[...], w_ref[...], preferred_element_type=F32)

    @pl.when(k == nk - 1)
    def _():
        o_ref[...] = x_ref[...] + gate_ref[0] * acc_ref[...]


def _mm2(a, w, x, gate, *, seq, tm=1024, tn=1024, tk=2048):
    N, K = a.shape
    D = w.shape[1]
    B = gate.shape[0]
    bpt = seq // tm
    nk = K // tk
    vmem = 2 * tm * tk * 2 + 2 * tk * tn * 2 + 5 * tm * tn * 4 + (6 << 20)
    return pl.pallas_call(
        functools.partial(_mm2_body, nk=nk),
        out_shape=jax.ShapeDtypeStruct((N, D), F32),
        grid=(N // tm, D // tn, nk),
        in_specs=[
            pl.BlockSpec((tm, tk), lambda i, j, k: (i, k)),
            pl.BlockSpec((tk, tn), lambda i, j, k: (k, j)),
            pl.BlockSpec((tm, tn), lambda i, j, k: (i, j)),
            pl.BlockSpec((1, 1, tn), lambda i, j, k: (i // bpt, 0, j)),
        ],
        out_specs=pl.BlockSpec((tm, tn), lambda i, j, k: (i, j)),
        scratch_shapes=[pltpu.VMEM((tm, tn), F32)],
        compiler_params=pltpu.CompilerParams(
            dimension_semantics=("parallel", "parallel", "arbitrary"),
            vmem_limit_bytes=_vmem_limit(vmem)),
        name="mm2_residual",
    )(a, w, x, gate.reshape(B, 1, D))


def _compress_body(x_ref, w1_ref, pe_ref, w2_ref, kg_ref, cos_ref, sin_ref, o_ref):
    kv = pl.program_id(0)
    tm = x_ref.shape[1]
    w1 = w1_ref[0]
    ab = jnp.dot(x_ref[0], w1, preferred_element_type=F32)
    peab = jnp.dot(pe_ref[0], w1, preferred_element_type=F32)
    pe_term = peab[0:1, :CMP_HIDDEN] + peab[1:2, CMP_HIDDEN:]
    nxt = pltpu.roll(ab[:, CMP_HIDDEN:], tm - 1, axis=0)
    pre = ab[:, :CMP_HIDDEN] + nxt + pe_term
    hid = pre * jax.nn.sigmoid(pre)
    cmp = jnp.dot(hid.astype(BF16), w2_ref[0], preferred_element_type=F32)

    @pl.when(kv == 0)
    def _():
        o_ref[0] = _norm_rope(cmp, kg_ref[...], cos_ref[...], sin_ref[...]).astype(BF16)

    @pl.when(kv == 1)
    def _():
        o_ref[0] = cmp.astype(BF16)


def _compress(chunks, w1cat, pe2, w2, kg, cos_c, sin_c, tm=1024):
    _, M, KC = chunks.shape
    H2 = w1cat.shape[2]
    tm = min(tm, M)
    return pl.pallas_call(
        _compress_body,
        out_shape=jax.ShapeDtypeStruct((2, M, HEAD_DIM), BF16),
        grid=(2, M // tm),
        in_specs=[
            pl.BlockSpec((1, tm, KC), lambda kv, i: (kv, i, 0)),
            pl.BlockSpec((1, KC, H2), lambda kv, i: (kv, 0, 0)),
            pl.BlockSpec((1, 8, KC), lambda kv, i: (kv, 0, 0)),
            pl.BlockSpec((1, CMP_HIDDEN, HEAD_DIM), lambda kv, i: (kv, 0, 0)),
            pl.BlockSpec((1, HEAD_DIM), lambda kv, i: (0, 0)),
            pl.BlockSpec((tm, HEAD_DIM), lambda kv, i: (i, 0)),
            pl.BlockSpec((tm, HEAD_DIM), lambda kv, i: (i, 0)),
        ],
        out_specs=pl.BlockSpec((1, tm, HEAD_DIM), lambda kv, i: (kv, i, 0)),
        compiler_params=pltpu.CompilerParams(
            dimension_semantics=("parallel", "parallel"),
            vmem_limit_bytes=_vmem_limit(2 * tm * KC * 2 + 2 * KC * H2 * 2 + (16 << 20))),
        name="nsa_compress",
    )(chunks, w1cat, pe2, w2, kg.reshape(1, HEAD_DIM), cos_c, sin_c)


def _flash_t(heads, v_get, bias_fn, lo, hi, tq, tk):
    nh = len(heads)
    for _, _, acc in heads:
        acc[...] = jnp.zeros_like(acc)

    def body(kt, carry):
        k0 = pl.multiple_of(kt * tk, tk)
        v = v_get(k0)
        bias = bias_fn(k0)
        out = []
        for h, (q, k_get, acc) in enumerate(heads):
            m, l = carry[2 * h], carry[2 * h + 1]
            s = lax.dot_general(k_get(k0), q, NT_DIMS, preferred_element_type=F32) + bias
            m_new = jnp.maximum(m, jnp.max(s, axis=0, keepdims=True))
            alpha = jnp.exp(m - m_new)
            p = jnp.exp(s - m_new)
            l_new = alpha * l + jnp.sum(p, axis=0, keepdims=True)
            pv = lax.dot_general(v, p.astype(BF16), TN_DIMS, preferred_element_type=F32)
            acc[...] = alpha * acc[...] + pv
            out += [m_new, l_new]
        return tuple(out)

    init = (jnp.full((1, tq), NEG, F32), jnp.zeros((1, tq), F32)) * nh
    fin = lax.fori_loop(lo, hi, body, init)
    return [heads[h][2][...] / fin[2 * h + 1] for h in range(nh)]


def _nsa_body(q_ref, ks_ref, kw_ref, vs_ref, vw_ref, kc_ref, vc_ref, gt_ref, et_ref, mt_ref,
              o_ref, acc_s, acc_w, *, tq, tk, n_sel, n_slc):
    qi = pl.program_id(2)
    t0 = qi * tq
    R = NSA_GROUP
    nc = kc_ref.shape[2]
    qs = [q_ref[0, :, h * HEAD_DIM:(h + 1) * HEAD_DIM] for h in range(R)]

    kc = kc_ref[0, 0]
    vc = vc_ref[0, 0]
    n_id = lax.broadcasted_iota(jnp.int32, (nc, tq), 0)
    t_id = t0 + lax.broadcasted_iota(jnp.int32, (nc, tq), 1)
    cm = n_id * CMP_STRIDE + (CMP_BLOCK - 1) <= t_id
    cbias = jnp.where(cm, 0.0, NEG)
    cmf = jnp.where(cm, 1.0, 0.0)
    o_cmp = []
    p_sum = jnp.zeros((nc, tq), F32)
    for h in range(R):
        s = lax.dot_general(kc, qs[h], NT_DIMS, preferred_element_type=F32) + cbias
        e = jnp.exp(s - jnp.max(s, axis=0, keepdims=True)) * cmf
        p = e / jnp.maximum(jnp.sum(e, axis=0, keepdims=True), 1e-30)
        o_cmp.append(lax.dot_general(vc, p.astype(BF16), TN_DIMS, preferred_element_type=F32))
        p_sum = p_sum + p

    p_hi = p_sum.astype(BF16)
    p_lo = (p_sum - p_hi.astype(F32)).astype(BF16)
    mt = mt_ref[...]
    imp = (jnp.dot(mt, p_hi, preferred_element_type=F32)
           + jnp.dot(mt, p_lo, preferred_element_type=F32))
    j_id = lax.broadcasted_iota(jnp.int32, (n_slc, tq), 0)
    tb = jnp.right_shift(t0 + lax.broadcasted_iota(jnp.int32, (n_slc, tq), 1),
                         SLC_BLOCK.bit_length() - 1)
    imp = jnp.where(j_id <= tb, imp, NEG)
    imp = jnp.where(j_id == 0, BIG, jnp.where(j_id == tb, BIG, jnp.where(j_id == tb - 1, BIG, imp)))
    rank = jnp.zeros((n_slc, tq), F32)
    for jp in range(n_slc):
        row = imp[jp:jp + 1, :]
        gt = jnp.where(row > imp, 1.0, 0.0)
        ge = jnp.where(row >= imp, 1.0, 0.0)
        rank = rank + jnp.where(j_id > jp, ge, gt)
    sel = jnp.where(rank < n_sel, 1.0, 0.0)
    sel = jnp.concatenate([sel, jnp.zeros((LANES - n_slc, tq), F32)], axis=0).astype(BF16)

    key_i = lax.broadcasted_iota(jnp.int32, (tk, tq), 0)
    t_i = t0 + lax.broadcasted_iota(jnp.int32, (tk, tq), 1)

    def slc_bias(k0):
        hit = jnp.dot(et_ref[pl.ds(k0, tk), :], sel, preferred_element_type=F32)
        return jnp.where(key_i + k0 <= t_i, jnp.where(hit > 0.5, 0.0, NEG), NEG)

    heads = [(qs[h], lambda k0: ks_ref[0, pl.ds(k0, tk), :], acc_s.at[h]) for h in range(R)]
    o_slc = _flash_t(heads, lambda k0: vs_ref[0, pl.ds(k0, tk), :], slc_bias,
                     0, (t0 + tq) // tk, tq, tk)

    def win_bias(k0):
        rel = t_i - (key_i + k0)
        return jnp.where(rel >= 0, jnp.where(rel < WIN, 0.0, NEG), NEG)

    heads = [(qs[h], lambda k0: kw_ref[0, pl.ds(k0, tk), :], acc_w.at[h]) for h in range(R)]
    lo = jnp.maximum(t0 - (WIN - 1), 0) // tk
    o_win = _flash_t(heads, lambda k0: vw_ref[0, pl.ds(k0, tk), :], win_bias,
                     lo, (t0 + tq) // tk, tq, tk)

    gt = gt_ref[0, 0]
    for h in range(R):
        o = (gt[h:h + 1, :] * o_cmp[h] + gt[R + h:R + h + 1, :] * o_slc[h]
             + gt[2 * R + h:2 * R + h + 1, :] * o_win[h])
        o_ref[0, :, h * HEAD_DIM:(h + 1) * HEAD_DIM] = o.T.astype(o_ref.dtype)


def _cmp_to_slc_map_t(n_cmp_pad, n_slc):
    cs = CMP_STRIDE * np.arange(n_cmp_pad)[None, :]
    ss = SLC_BLOCK * np.arange(n_slc)[:, None]
    ov = np.minimum(cs + CMP_BLOCK, ss + SLC_BLOCK) - np.maximum(cs, ss)
    return (np.clip(ov, 0, None) / CMP_STRIDE).astype(np.float32)


def _nsa_attention(proj, cmp_kv, gates_t, B, T, tq=256, tk=256):
    G, R = NSA_KV_HEADS, NSA_GROUP
    nc = T // CMP_STRIDE
    n_slc = T // SLC_BLOCK
    n_sel = min(SLC_TOPN, n_slc)
    qw = R * HEAD_DIM
    kb = (G * qw) // HEAD_DIM
    et = (np.arange(T)[:, None] // SLC_BLOCK == np.arange(LANES)[None, :]).astype(np.float32)
    mt = _cmp_to_slc_map_t(nc, n_slc)
    kv_spec = lambda off: pl.BlockSpec((1, T, HEAD_DIM), lambda b, g, i: (b, 0, off + g))
    return pl.pallas_call(
        functools.partial(_nsa_body, tq=tq, tk=tk, n_sel=n_sel, n_slc=n_slc),
        out_shape=jax.ShapeDtypeStruct((B, T, G * qw), BF16),
        grid=(B, G, T // tq),
        in_specs=[
            pl.BlockSpec((1, tq, qw), lambda b, g, i: (b, i, g)),
            kv_spec(kb), kv_spec(kb + G), kv_spec(kb + 4 * G), kv_spec(kb + 5 * G),
            pl.BlockSpec((1, 1, nc, HEAD_DIM), lambda b, g, i: (0, b * G + g, 0, 0)),
            pl.BlockSpec((1, 1, nc, HEAD_DIM), lambda b, g, i: (1, b * G + g, 0, 0)),
            pl.BlockSpec((1, 1, 16, tq), lambda b, g, i: (b, g, 0, i)),
            pl.BlockSpec((T, LANES), lambda b, g, i: (0, 0)),
            pl.BlockSpec((n_slc, nc), lambda b, g, i: (0, 0)),
        ],
        out_specs=pl.BlockSpec((1, tq, qw), lambda b, g, i: (b, i, g)),
        scratch_shapes=[pltpu.VMEM((R, HEAD_DIM, tq), F32), pltpu.VMEM((R, HEAD_DIM, tq), F32)],
        compiler_params=pltpu.CompilerParams(
            dimension_semantics=("parallel", "parallel", "arbitrary"),
            vmem_limit_bytes=_vmem_limit(48 << 20)),
        name="nsa_attention",
    )(proj, proj, proj, proj, proj, cmp_kv, cmp_kv, gates_t,
      jnp.asarray(et, BF16), jnp.asarray(mt, BF16))


def _diff_body(q_ref, k_ref, v_ref, lam_ref, sg_ref, o_ref, acc, *, tq, tk, lam_init):
    qi = pl.program_id(2)
    t0 = qi * tq
    dv = 2 * HEAD_DIM
    key_i = lax.broadcasted_iota(jnp.int32, (tk, tq), 0)
    t_i = t0 + lax.broadcasted_iota(jnp.int32, (tk, tq), 1)

    def bias(k0):
        return jnp.where(key_i + k0 <= t_i, 0.0, NEG)

    def k_get(c):
        return lambda k0: k_ref[0, pl.ds(k0, tk), c * HEAD_DIM:(c + 1) * HEAD_DIM]

    heads = [(q_ref[0, :, c * HEAD_DIM:(c + 1) * HEAD_DIM], k_get(c), acc.at[c]) for c in range(2)]
    o1, o2 = _flash_t(heads, lambda k0: v_ref[0, pl.ds(k0, tk), :], bias,
                      0, (t0 + tq) // tk, tq, tk)

    lp = lam_ref[...]
    lam = (jnp.exp(jnp.sum(lp[0:1] * lp[1:2], axis=-1, keepdims=True))
           - jnp.exp(jnp.sum(lp[2:3] * lp[3:4], axis=-1, keepdims=True)) + lam_init)
    o = o1 - lam * o2
    ms = jnp.mean(o * o, axis=0, keepdims=True)
    o = o * lax.rsqrt(ms + NORM_EPS)
    sg = sg_ref[...] * (1.0 - lam_init)
    for c in range(dv // HEAD_DIM):
        cols = slice(c * HEAD_DIM, (c + 1) * HEAD_DIM)
        o_ref[0, :, cols] = (o[cols, :].T * sg[:, cols]).astype(o_ref.dtype)


def _diff_attention(proj, lam_p, sub_g, lam_init, B, T, tq=256, tk=256):
    dv = 2 * HEAD_DIM
    H = proj.shape[2] // (3 * dv)
    return pl.pallas_call(
        functools.partial(_diff_body, tq=tq, tk=tk, lam_init=lam_init),
        out_shape=jax.ShapeDtypeStruct((B, T, H * dv), BF16),
        grid=(B, H, T // tq),
        in_specs=[
            pl.BlockSpec((1, tq, dv), lambda b, h, i: (b, i, h)),
            pl.BlockSpec((1, T, dv), lambda b, h, i: (b, 0, H + h)),
            pl.BlockSpec((1, T, dv), lambda b, h, i: (b, 0, 2 * H + h)),
            pl.BlockSpec((4, HEAD_DIM), lambda b, h, i: (0, 0)),
            pl.BlockSpec((1, dv), lambda b, h, i: (0, 0)),
        ],
        out_specs=pl.BlockSpec((1, tq, dv), lambda b, h, i: (b, i, h)),
        scratch_shapes=[pltpu.VMEM((2, dv, tq), F32)],
        compiler_params=pltpu.CompilerParams(
            dimension_semantics=("parallel", "parallel", "arbitrary"),
            vmem_limit_bytes=_vmem_limit(48 << 20)),
        name="diff_attention",
    )(proj, proj, proj, lam_p, sub_g.reshape(1, dv))


def _nsa_layer(x, g_norm, sc, sh, gate, w_in, w_out, q_g, k_g, cmp_pos, cmp_w1, cmp_w2,
               cos_t, sin_t, B, T):
    G, R, dh = NSA_KV_HEADS, NSA_GROUP, HEAD_DIM
    H = G * R
    qd, gd = H * dh, G * dh
    scale = dh ** -0.5
    cut = lambda n: w_in[:, qd + n * gd: qd + (n + 1) * gd]
    w_main = jnp.concatenate([w_in[:, :qd], cut(2), cut(4), cut(0), cut(1), cut(3), cut(5)],
                             axis=1).astype(BF16)
    w_gate = jnp.pad(w_in[:, qd + 6 * gd:], ((0, 0), (0, LANES - 3 * H))).astype(BF16)
    colgain = jnp.concatenate([jnp.tile(q_g * scale, H), jnp.tile(k_g[1], G), jnp.tile(k_g[2], G),
                               jnp.ones((4 * gd,), F32)])
    proj = _mm1(x, g_norm, sc, sh, w_main, epilogue="rope", out_dtype=BF16, seq=T,
                rope=(colgain, cos_t, sin_t), n_rope=(qd + 2 * gd) // 1024)
    gates = _mm1(x, g_norm, sc, sh, w_gate, epilogue="sigmoid", out_dtype=F32, seq=T, tn=LANES)
    proj = proj.reshape(B, T, -1)

    nc = T // CMP_STRIDE
    kcvc = proj[:, :, qd + 2 * gd: qd + 4 * gd].reshape(B, nc, CMP_STRIDE, 2, G, dh)
    chunks = kcvc.transpose(3, 0, 4, 1, 2, 5).reshape(2, B * G * nc, CMP_STRIDE * dh)
    half = CMP_BLOCK // 2
    w1cat = jnp.concatenate([cmp_w1[:, :half].reshape(2, half * dh, CMP_HIDDEN),
                             cmp_w1[:, half:].reshape(2, half * dh, CMP_HIDDEN)], axis=2).astype(BF16)
    pe2 = jnp.stack([cmp_pos[:, :half].reshape(2, half * dh),
                     cmp_pos[:, half:].reshape(2, half * dh)], axis=1)
    pe2 = jnp.pad(pe2, ((0, 0), (0, 6), (0, 0))).astype(BF16)
    last = np.minimum(CMP_STRIDE * np.arange(nc) + CMP_BLOCK - 1, T - 1)
    tab = lambda t: jnp.broadcast_to(t.reshape(B, 1, T, dh)[:, :, last], (B, G, nc, dh)).reshape(-1, dh)
    cmp_kv = _compress(chunks, w1cat, pe2, cmp_w2.astype(BF16), k_g[0], tab(cos_t), tab(sin_t))
    cmp_kv = cmp_kv.reshape(2, B * G, nc, dh)

    gates_t = gates[:, :3 * H].reshape(B, T, 3, G, R).transpose(0, 3, 2, 4, 1).reshape(B, G, 3 * R, T)
    gates_t = jnp.pad(gates_t, ((0, 0), (0, 0), (0, 16 - 3 * R), (0, 0)))
    o = _nsa_attention(proj, cmp_kv, gates_t, B, T)
    return _mm2(o.reshape(B * T, qd), w_out.astype(BF16), x, gate, seq=T)


def _diff_layer(x, g_norm, sc, sh, gate, w_in, w_out, q_g, k_g, lam_p, sub_g, lam_init,
                cos_t, sin_t, B, T):
    dh = HEAD_DIM
    D = w_out.shape[0]
    scale = dh ** -0.5
    nh = D // dh
    colgain = jnp.concatenate([jnp.tile(q_g * scale, nh), jnp.tile(k_g, nh), jnp.ones((D,), F32)])
    proj = _mm1(x, g_norm, sc, sh, w_in.astype(BF16), epilogue="rope", out_dtype=BF16, seq=T,
                rope=(colgain, cos_t, sin_t), n_rope=2 * D // 1024)
    o = _diff_attention(proj.reshape(B, T, -1), lam_p, sub_g, lam_init, B, T)
    return _mm2(o.reshape(B * T, D), w_out.astype(BF16), x, gate, seq=T)


def kernel(x, c, positions, ada_w, ada_b, attn_norm_g, mlp_norm_g, mlp_w1, mlp_w2, nsa_w_in, nsa_w_out, nsa_q_norm, nsa_k_norm, nsa_cmp_pos, nsa_cmp_w1, nsa_cmp_w2, diff_w_in, diff_w_out, diff_q_norm, diff_k_norm, diff_lambda, diff_sub_norm):
    B, T, D = x.shape
    depth = ada_w.shape[0]
    mod_all = _adaln(c, ada_w, ada_b)
    cos_t, sin_t = _rope_tables(positions)
    xf = x.reshape(B * T, D)
    for i in range(depth):
        sh1, sc1, g1, sh2, sc2, g2 = [mod_all[i, :, n * D:(n + 1) * D] for n in range(6)]
        j = i // 2
        if i % 2 == 0:
            xf = _nsa_layer(xf, attn_norm_g[i], sc1, sh1, g1, nsa_w_in[j], nsa_w_out[j],
                            nsa_q_norm[j], nsa_k_norm[j], nsa_cmp_pos[j], nsa_cmp_w1[j],
                            nsa_cmp_w2[j], cos_t, sin_t, B, T)
        else:
            lam_init = 0.8 - 0.6 * math.exp(-0.3 * i)
            xf = _diff_layer(xf, attn_norm_g[i], sc1, sh1, g1, diff_w_in[j], diff_w_out[j],
                             diff_q_norm[j], diff_k_norm[j], diff_lambda[j], diff_sub_norm[j],
                             lam_init, cos_t, sin_t, B, T)
        hid = _mm1(xf, mlp_norm_g[i], sc2, sh2, mlp_w1[i].astype(BF16), epilogue="sqrelu",
                   out_dtype=BF16, seq=T)
        xf = _mm2(hid, mlp_w2[i].astype(BF16), xf, g2, seq=T)
    return xf.reshape(B, T, D)
```

```python
import functools
import math

import numpy as np
import jax
import jax.numpy as jnp
from jax import lax
from jax.experimental import pallas as pl
from jax.experimental.pallas import tpu as pltpu

F32 = jnp.float32
BF16 = jnp.bfloat16

HEAD_DIM = 128
ROPE_THETA = 10000.0
NORM_EPS = 1e-6
NEG = -1e30
BIG = 1e30
LOG2E = math.log2(math.e)

NSA_KV_HEADS = 4
NSA_GROUP = 4
CMP_BLOCK = 32
CMP_STRIDE = 16
CMP_HIDDEN = 2 * HEAD_DIM
SLC_BLOCK = 64
SLC_TOPN = 16
WIN = 512

LANES = 128
V7X_VMEM_BYTES = 64 * 1024 * 1024
VMEM_CAP = V7X_VMEM_BYTES - 8 * 1024 * 1024

NT_DIMS = (((1,), (1,)), ((), ()))
TN_DIMS = (((0,), (0,)), ((), ()))


def _vmem_limit(nbytes):
    return int(min(VMEM_CAP, max(32 * 1024 * 1024, nbytes)))


def _adaln_body(c_ref, w_ref, b_ref, o_ref):
    c = c_ref[...]
    sc = (c * jax.nn.sigmoid(c)).astype(BF16)
    w = w_ref[0].astype(BF16)
    o_ref[0] = jnp.dot(sc, w, preferred_element_type=F32) + b_ref[0]


def _adaln(c, ada_w, ada_b, tn=1024):
    L, D, E = ada_w.shape
    B = c.shape[0]
    return pl.pallas_call(
        _adaln_body,
        out_shape=jax.ShapeDtypeStruct((L, B, E), F32),
        grid=(L, E // tn),
        in_specs=[
            pl.BlockSpec((B, D), lambda l, j: (0, 0)),
            pl.BlockSpec((1, D, tn), lambda l, j: (l, 0, j)),
            pl.BlockSpec((1, 1, tn), lambda l, j: (l, 0, j)),
        ],
        out_specs=pl.BlockSpec((1, B, tn), lambda l, j: (l, 0, j)),
        compiler_params=pltpu.CompilerParams(
            dimension_semantics=("parallel", "parallel"),
            vmem_limit_bytes=_vmem_limit(2 * D * tn * 4 + D * tn * 2 + (8 << 20))),
        name="adaln_mod",
    )(c, ada_w, ada_b.reshape(L, 1, E))


def _rope_tab_body(pos_ref, inv_ref, sgn_ref, cos_ref, sin_ref):
    ang = pos_ref[...] * inv_ref[...]
    cos_ref[...] = jnp.cos(ang)
    sin_ref[...] = jnp.sin(ang) * sgn_ref[...]


def _rope_tables(positions, tr=1024):
    n = positions.size
    half = HEAD_DIM // 2
    inv = ROPE_THETA ** (-jnp.arange(half, dtype=F32) / half)
    inv = jnp.concatenate([inv, inv]).reshape(1, HEAD_DIM)
    sgn = jnp.concatenate([-jnp.ones((half,), F32), jnp.ones((half,), F32)]).reshape(1, HEAD_DIM)
    posb = jnp.broadcast_to(positions.astype(F32).reshape(n, 1), (n, HEAD_DIM))
    row = pl.BlockSpec((tr, HEAD_DIM), lambda i: (i, 0))
    vec = pl.BlockSpec((1, HEAD_DIM), lambda i: (0, 0))
    return pl.pallas_call(
        _rope_tab_body,
        out_shape=(jax.ShapeDtypeStruct((n, HEAD_DIM), F32),) * 2,
        grid=(n // tr,),
        in_specs=[row, vec, vec],
        out_specs=(row, row),
        compiler_params=pltpu.CompilerParams(dimension_semantics=("parallel",)),
        name="rope_tables",
    )(posb, inv, sgn)


def _norm_rope(y, gain, cosv, sinv):
    ms = jnp.mean(y * y, axis=-1, keepdims=True)
    y = y * lax.rsqrt(ms + NORM_EPS) * gain
    return y * cosv + pltpu.roll(y, HEAD_DIM // 2, axis=1) * sinv


def _mm1_body(x_ref, g_ref, sc_ref, sh_ref, w_ref, *refs, epilogue, n_rope, rc):
    if epilogue == "rope":
        cg_ref, cos_ref, sin_ref, o_ref, h_scr = refs
    else:
        o_ref, h_scr = refs
    j = pl.program_id(1)
    tm = x_ref.shape[0]
    tn = w_ref.shape[1]
    chunks = [slice(r * rc, (r + 1) * rc) for r in range(tm // rc)]

    @pl.when(j == 0)
    def _():
        for rows in chunks:
            x = x_ref[rows, :]
            ms = jnp.mean(x * x, axis=-1, keepdims=True)
            y = x * lax.rsqrt(ms + NORM_EPS) * g_ref[...]
            h_scr[rows, :] = (y * (1.0 + sc_ref[0]) + sh_ref[0]).astype(BF16)

    def compute(rope):
        for rows in chunks:
            acc = jnp.dot(h_scr[rows, :], w_ref[...], preferred_element_type=F32)
            if epilogue == "sqrelu":
                r = jnp.maximum(acc, 0.0)
                o_ref[rows, :] = (r * r).astype(o_ref.dtype)
            elif epilogue == "sigmoid":
                o_ref[rows, :] = jax.nn.sigmoid(acc).astype(o_ref.dtype)
            elif rope:
                cosv = cos_ref[rows, :]
                sinv = sin_ref[rows, :]
                for h in range(tn // HEAD_DIM):
                    cols = slice(h * HEAD_DIM, (h + 1) * HEAD_DIM)
                    y = _norm_rope(acc[:, cols], cg_ref[:, cols], cosv, sinv)
                    o_ref[rows, cols] = y.astype(o_ref.dtype)
            else:
                o_ref[rows, :] = acc.astype(o_ref.dtype)

    if epilogue == "rope":
        pl.when(j < n_rope)(functools.partial(compute, True))
        pl.when(j >= n_rope)(functools.partial(compute, False))
    else:
        compute(False)


def _mm1(x, g, sc, sh, w, *, epilogue, out_dtype, seq, rope=None, n_rope=0,
         tm=1024, tn=1024, rc=256):
    N, D = x.shape
    Nout = w.shape[1]
    tn = min(tn, Nout)
    B = sc.shape[0]
    bpt = seq // tm
    in_specs = [
        pl.BlockSpec((tm, D), lambda i, j: (i, 0)),
        pl.BlockSpec((1, D), lambda i, j: (0, 0)),
        pl.BlockSpec((1, 1, D), lambda i, j: (i // bpt, 0, 0)),
        pl.BlockSpec((1, 1, D), lambda i, j: (i // bpt, 0, 0)),
        pl.BlockSpec((D, tn), lambda i, j: (0, j)),
    ]
    args = [x, g.reshape(1, D), sc.reshape(B, 1, D), sh.reshape(B, 1, D), w]
    if epilogue == "rope":
        colgain, cos_t, sin_t = rope
        in_specs += [
            pl.BlockSpec((1, tn), lambda i, j: (0, j)),
            pl.BlockSpec((tm, HEAD_DIM), lambda i, j: (i, 0)),
            pl.BlockSpec((tm, HEAD_DIM), lambda i, j: (i, 0)),
        ]
        args += [colgain.reshape(1, Nout), cos_t, sin_t]
    osz = jnp.dtype(out_dtype).itemsize
    vmem = (2 * tm * D * 4 + 2 * D * tn * 2 + 2 * tm * tn * osz + tm * D * 2
            + 4 * tm * HEAD_DIM * 4 + 4 * rc * tn * 4 + (6 << 20))
    return pl.pallas_call(
        functools.partial(_mm1_body, epilogue=epilogue, n_rope=n_rope, rc=rc),
        out_shape=jax.ShapeDtypeStruct((N, Nout), out_dtype),
        grid=(N // tm, Nout // tn),
        in_specs=in_specs,
        out_specs=pl.BlockSpec((tm, tn), lambda i, j: (i, j)),
        scratch_shapes=[pltpu.VMEM((tm, D), BF16)],
        compiler_params=pltpu.CompilerParams(
            dimension_semantics=("parallel", "arbitrary"),
            vmem_limit_bytes=_vmem_limit(vmem)),
        name="mm1_" + epilogue,
    )(*args)


def _mm2_body(a_ref, w_ref, x_ref, gate_ref, o_ref, acc_ref, *, nk):
    k = pl.program_id(2)

    @pl.when(k == 0)
    def _():
        acc_ref[...] = jnp.zeros_like(acc_ref)

    acc_ref[...] += jnp.dot(a_ref[...], w_ref[...], preferred_element_type=F32)

    @pl.when(k == nk - 1)
    def _():
        o_ref[...] = x_ref[...] + gate_ref[0] * acc_ref[...]


def _mm2(a, w, x, gate, *, seq, tm=1024, tn=1024, tk=2048):
    N, K = a.shape
    D = w.shape[1]
    B = gate.shape[0]
    bpt = seq // tm
    nk = K // tk
    vmem = 2 * tm * tk * 2 + 2 * tk * tn * 2 + 5 * tm * tn * 4 + (6 << 20)
    return pl.pallas_call(
        functools.partial(_mm2_body, nk=nk),
        out_shape=jax.ShapeDtypeStruct((N, D), F32),
        grid=(N // tm, D // tn, nk),
        in_specs=[
            pl.BlockSpec((tm, tk), lambda i, j, k: (i, k)),
            pl.BlockSpec((tk, tn), lambda i, j, k: (k, j)),
            pl.BlockSpec((tm, tn), lambda i, j, k: (i, j)),
            pl.BlockSpec((1, 1, tn), lambda i, j, k: (i // bpt, 0, j)),
        ],
        out_specs=pl.BlockSpec((tm, tn), lambda i, j, k: (i, j)),
        scratch_shapes=[pltpu.VMEM((tm, tn), F32)],
        compiler_params=pltpu.CompilerParams(
            dimension_semantics=("parallel", "parallel", "arbitrary"),
            vmem_limit_bytes=_vmem_limit(vmem)),
        name="mm2_residual",
    )(a, w, x, gate.reshape(B, 1, D))


def _compress_body(x_ref, w1_ref, pe_ref, w2_ref, kg_ref, cos_ref, sin_ref, o_ref):
    kv = pl.program_id(0)
    tm = x_ref.shape[1]
    w1 = w1_ref[0]
    ab = jnp.dot(x_ref[0], w1, preferred_element_type=F32)
    peab = jnp.dot(pe_ref[0], w1, preferred_element_type=F32)
    pe_term = peab[0:1, :CMP_HIDDEN] + peab[1:2, CMP_HIDDEN:]
    nxt = pltpu.roll(ab[:, CMP_HIDDEN:], tm - 1, axis=0)
    pre = ab[:, :CMP_HIDDEN] + nxt + pe_term
    hid = pre * jax.nn.sigmoid(pre)
    cmp = jnp.dot(hid.astype(BF16), w2_ref[0], preferred_element_type=F32)

    @pl.when(kv == 0)
    def _():
        o_ref[0] = _norm_rope(cmp, kg_ref[...], cos_ref[...], sin_ref[...]).astype(BF16)

    @pl.when(kv == 1)
    def _():
        o_ref[0] = cmp.astype(BF16)


def _compress(chunks, w1cat, pe2, w2, kg, cos_c, sin_c, tm=1024):
    _, M, KC = chunks.shape
    H2 = w1cat.shape[2]
    tm = min(tm, M)
    return pl.pallas_call(
        _compress_body,
        out_shape=jax.ShapeDtypeStruct((2, M, HEAD_DIM), BF16),
        grid=(2, M // tm),
        in_specs=[
            pl.BlockSpec((1, tm, KC), lambda kv, i: (kv, i, 0)),
            pl.BlockSpec((1, KC, H2), lambda kv, i: (kv, 0, 0)),
            pl.BlockSpec((1, 8, KC), lambda kv, i: (kv, 0, 0)),
            pl.BlockSpec((1, CMP_HIDDEN, HEAD_DIM), lambda kv, i: (kv, 0, 0)),
            pl.BlockSpec((1, HEAD_DIM), lambda kv, i: (0, 0)),
            pl.BlockSpec((tm, HEAD_DIM), lambda kv, i: (i, 0)),
            pl.BlockSpec((tm, HEAD_DIM), lambda kv, i: (i, 0)),
        ],
        out_specs=pl.BlockSpec((1, tm, HEAD_DIM), lambda kv, i: (kv, i, 0)),
        compiler_params=pltpu.CompilerParams(
            dimension_semantics=("parallel", "parallel"),
            vmem_limit_bytes=_vmem_limit(2 * tm * KC * 2 + 2 * KC * H2 * 2 + (16 << 20))),
        name="nsa_compress",
    )(chunks, w1cat, pe2, w2, kg.reshape(1, HEAD_DIM), cos_c, sin_c)


def _online_update(ss, v, stats, accs):
    out = []
    for s, (m, l), acc in zip(ss, stats, accs):
        m_new = jnp.maximum(m, jnp.max(s, axis=0, keepdims=True))
        alpha = jnp.exp2(m - m_new)
        p = jnp.exp2(s - m_new)
        l_new = alpha * l + jnp.sum(p, axis=0, keepdims=True)
        pv = lax.dot_general(v, p.astype(BF16), TN_DIMS, preferred_element_type=F32)
        acc[...] = alpha * acc[...] + pv
        out.append((m_new, l_new))
    return out


def _causal_sweep(tile_fn, accs, qi, tq, causal):
    n = len(accs)
    for acc in accs:
        acc[...] = jnp.zeros_like(acc)

    def body(kt, carry):
        stats = [(carry[2 * c], carry[2 * c + 1]) for c in range(n)]
        new = tile_fn(pl.multiple_of(kt * tq, tq), stats, None)
        return tuple(x for pair in new for x in pair)

    init = (jnp.full((1, tq), NEG, F32), jnp.zeros((1, tq), F32)) * n
    carry = lax.fori_loop(0, qi, body, init)
    stats = [(carry[2 * c], carry[2 * c + 1]) for c in range(n)]
    stats = tile_fn(pl.multiple_of(qi * tq, tq), stats, causal)
    return [accs[c][...] / stats[c][1] for c in range(n)]


def _nsa_body(q_ref, ks_ref, kw_ref, vs_ref, vw_ref, kc_ref, vc_ref, gt_ref, et_ref, mt_ref,
              o_ref, qa_scr, acc_s, *, tq, n_sel, n_slc):
    qi = pl.program_id(2)
    t0 = pl.multiple_of(qi * tq, tq)
    R = NSA_GROUP
    nc = kc_ref.shape[2]
    qh = lambda h: q_ref[0, :, h * HEAD_DIM:(h + 1) * HEAD_DIM]

    kc = kc_ref[0, 0]
    vc = vc_ref[0, 0]
    n_id = lax.broadcasted_iota(jnp.int32, (nc, tq), 0)
    t_id = t0 + lax.broadcasted_iota(jnp.int32, (nc, tq), 1)
    cm = n_id * CMP_STRIDE + (CMP_BLOCK - 1) <= t_id
    cbias = jnp.where(cm, 0.0, NEG)
    cmf = jnp.where(cm, 1.0, 0.0)
    o_cmp = []
    p_sum = jnp.zeros((nc, tq), F32)
    for h in range(R):
        s = lax.dot_general(kc, qh(h), NT_DIMS, preferred_element_type=F32) + cbias
        e = jnp.exp2(s - jnp.max(s, axis=0, keepdims=True)) * cmf
        p = e / jnp.maximum(jnp.sum(e, axis=0, keepdims=True), 1e-30)
        o_cmp.append(lax.dot_general(vc, p.astype(BF16), TN_DIMS, preferred_element_type=F32))
        p_sum = p_sum + p

    p_hi = p_sum.astype(BF16)
    p_lo = (p_sum - p_hi.astype(F32)).astype(BF16)
    mt = mt_ref[...]
    imp = (jnp.dot(mt, p_hi, preferred_element_type=F32)
           + jnp.dot(mt, p_lo, preferred_element_type=F32))
    j_id = lax.broadcasted_iota(jnp.int32, (n_slc, tq), 0)
    tb = jnp.right_shift(t0 + lax.broadcasted_iota(jnp.int32, (n_slc, tq), 1),
                         SLC_BLOCK.bit_length() - 1)
    imp = jnp.where(j_id <= tb, imp, NEG)
    imp = jnp.where(j_id == 0, BIG, jnp.where(j_id == tb, BIG, jnp.where(j_id == tb - 1, BIG, imp)))
    rank = jnp.zeros((n_slc, tq), F32)
    for jp in range(n_slc):
        row = imp[jp:jp + 1, :]
        gt = jnp.where(row > imp, 1.0, 0.0)
        ge = jnp.where(row >= imp, 1.0, 0.0)
        rank = rank + jnp.where(j_id > jp, ge, gt)
    selb = jnp.where(rank < n_sel, 0.0, NEG)
    selb = jnp.concatenate([selb, jnp.zeros((LANES - n_slc, tq), F32)], axis=0).T.astype(BF16)
    for h in range(R):
        qa_scr[h] = jnp.concatenate([qh(h), selb], axis=1)

    row_i = lax.broadcasted_iota(jnp.int32, (tq, tq), 0)
    col_i = lax.broadcasted_iota(jnp.int32, (tq, tq), 1)
    causal = jnp.where(row_i <= col_i, 0.0, NEG)

    def slc_tile(k0, stats, bias):
        k_aug = jnp.concatenate([ks_ref[0, pl.ds(k0, tq), :], et_ref[pl.ds(k0, tq), :]], axis=1)
        ss = [lax.dot_general(k_aug, qa_scr[h], NT_DIMS, preferred_element_type=F32)
              for h in range(R)]
        if bias is not None:
            ss = [s + bias for s in ss]
        return _online_update(ss, vs_ref[0, pl.ds(k0, tq), :], stats,
                              [acc_s.at[h] for h in range(R)])

    o_slc = _causal_sweep(slc_tile, [acc_s.at[h] for h in range(R)], qi, tq, causal)

    kp = pl.multiple_of(jnp.maximum(qi - 1, 0) * tq, tq)
    prev_bias = jnp.where(row_i > col_i, 0.0, NEG) + jnp.where(qi > 0, 0.0, NEG)
    wbias = jnp.concatenate([prev_bias, causal], axis=0)
    kw2 = jnp.concatenate([kw_ref[0, pl.ds(kp, tq), :], kw_ref[0, pl.ds(t0, tq), :]], axis=0)
    vw2 = jnp.concatenate([vw_ref[0, pl.ds(kp, tq), :], vw_ref[0, pl.ds(t0, tq), :]], axis=0)
    o_win = []
    for h in range(R):
        s = lax.dot_general(kw2, qh(h), NT_DIMS, preferred_element_type=F32) + wbias
        p = jnp.exp2(s - jnp.max(s, axis=0, keepdims=True))
        l = jnp.sum(p, axis=0, keepdims=True)
        o_win.append(lax.dot_general(vw2, p.astype(BF16), TN_DIMS, preferred_element_type=F32) / l)

    gt = gt_ref[0, 0]
    for h in range(R):
        o = (gt[h:h + 1, :] * o_cmp[h] + gt[R + h:R + h + 1, :] * o_slc[h]
             + gt[2 * R + h:2 * R + h + 1, :] * o_win[h])
        o_ref[0, :, h * HEAD_DIM:(h + 1) * HEAD_DIM] = o.T.astype(o_ref.dtype)


def _cmp_to_slc_map_t(n_cmp_pad, n_slc):
    cs = CMP_STRIDE * np.arange(n_cmp_pad)[None, :]
    ss = SLC_BLOCK * np.arange(n_slc)[:, None]
    ov = np.minimum(cs + CMP_BLOCK, ss + SLC_BLOCK) - np.maximum(cs, ss)
    return (np.clip(ov, 0, None) / CMP_STRIDE).astype(np.float32)


def _nsa_attention(proj, cmp_kv, gates_t, B, T):
    G, R = NSA_KV_HEADS, NSA_GROUP
    tq = WIN
    nc = T // CMP_STRIDE
    n_slc = T // SLC_BLOCK
    n_sel = min(SLC_TOPN, n_slc)
    qw = R * HEAD_DIM
    kb = (G * qw) // HEAD_DIM
    et = (np.arange(T)[:, None] // SLC_BLOCK == np.arange(LANES)[None, :]).astype(np.float32)
    mt = _cmp_to_slc_map_t(nc, n_slc)
    kv_spec = lambda off: pl.BlockSpec((1, T, HEAD_DIM), lambda b, g, i: (b, 0, off + g))
    return pl.pallas_call(
        functools.partial(_nsa_body, tq=tq, n_sel=n_sel, n_slc=n_slc),
        out_shape=jax.ShapeDtypeStruct((B, T, G * qw), BF16),
        grid=(B, G, T // tq),
        in_specs=[
            pl.BlockSpec((1, tq, qw), lambda b, g, i: (b, i, g)),
            kv_spec(kb), kv_spec(kb + G), kv_spec(kb + 4 * G), kv_spec(kb + 5 * G),
            pl.BlockSpec((1, 1, nc, HEAD_DIM), lambda b, g, i: (0, b * G + g, 0, 0)),
            pl.BlockSpec((1, 1, nc, HEAD_DIM), lambda b, g, i: (1, b * G + g, 0, 0)),
            pl.BlockSpec((1, 1, 16, tq), lambda b, g, i: (b, g, 0, i)),
            pl.BlockSpec((T, LANES), lambda b, g, i: (0, 0)),
            pl.BlockSpec((n_slc, nc), lambda b, g, i: (0, 0)),
        ],
        out_specs=pl.BlockSpec((1, tq, qw), lambda b, g, i: (b, i, g)),
        scratch_shapes=[pltpu.VMEM((R, tq, 2 * HEAD_DIM), BF16),
                        pltpu.VMEM((R, HEAD_DIM, tq), F32)],
        compiler_params=pltpu.CompilerParams(
            dimension_semantics=("parallel", "parallel", "arbitrary"),
            vmem_limit_bytes=_vmem_limit(48 << 20)),
        name="nsa_attention",
    )(proj, proj, proj, proj, proj, cmp_kv, cmp_kv, gates_t,
      jnp.asarray(et, BF16), jnp.asarray(mt, BF16))


def _diff_body(q_ref, k_ref, v_ref, lam_ref, sg_ref, o_ref, acc, *, tq, lam_init):
    qi = pl.program_id(2)
    dv = 2 * HEAD_DIM
    row_i = lax.broadcasted_iota(jnp.int32, (tq, tq), 0)
    col_i = lax.broadcasted_iota(jnp.int32, (tq, tq), 1)
    causal = jnp.where(row_i <= col_i, 0.0, NEG)

    def tile(k0, stats, bias):
        ss = [lax.dot_general(k_ref[0, pl.ds(k0, tq), c * HEAD_DIM:(c + 1) * HEAD_DIM],
                              q_ref[0, :, c * HEAD_DIM:(c + 1) * HEAD_DIM],
                              NT_DIMS, preferred_element_type=F32) for c in range(2)]
        if bias is not None:
            ss = [s + bias for s in ss]
        return _online_update(ss, v_ref[0, pl.ds(k0, tq), :], stats, [acc.at[0], acc.at[1]])

    o1, o2 = _causal_sweep(tile, [acc.at[0], acc.at[1]], qi, tq, causal)

    lp = lam_ref[...]
    lam = (jnp.exp(jnp.sum(lp[0:1] * lp[1:2], axis=-1, keepdims=True))
           - jnp.exp(jnp.sum(lp[2:3] * lp[3:4], axis=-1, keepdims=True)) + lam_init)
    o = o1 - lam * o2
    ms = jnp.mean(o * o, axis=0, keepdims=True)
    o = o * lax.rsqrt(ms + NORM_EPS)
    sg = sg_ref[...] * (1.0 - lam_init)
    for c in range(dv // HEAD_DIM):
        cols = slice(c * HEAD_DIM, (c + 1) * HEAD_DIM)
        o_ref[0, :, cols] = (o[cols, :].T * sg[:, cols]).astype(o_ref.dtype)


def _diff_attention(proj, lam_p, sub_g, lam_init, B, T, tq=512):
    dv = 2 * HEAD_DIM
    H = proj.shape[2] // (3 * dv)
    return pl.pallas_call(
        functools.partial(_diff_body, tq=tq, lam_init=lam_init),
        out_shape=jax.ShapeDtypeStruct((B, T, H * dv), BF16),
        grid=(B, H, T // tq),
        in_specs=[
            pl.BlockSpec((1, tq, dv), lambda b, h, i: (b, i, h)),
            pl.BlockSpec((1, T, dv), lambda b, h, i: (b, 0, H + h)),
            pl.BlockSpec((1, T, dv), lambda b, h, i: (b, 0, 2 * H + h)),
            pl.BlockSpec((4, HEAD_DIM), lambda b, h, i: (0, 0)),
            pl.BlockSpec((1, dv), lambda b, h, i: (0, 0)),
        ],
        out_specs=pl.BlockSpec((1, tq, dv), lambda b, h, i: (b, i, h)),
        scratch_shapes=[pltpu.VMEM((2, dv, tq), F32)],
        compiler_params=pltpu.CompilerParams(
            dimension_semantics=("parallel", "parallel", "arbitrary"),
            vmem_limit_bytes=_vmem_limit(48 << 20)),
        name="diff_attention",
    )(proj, proj, proj, lam_p, sub_g.reshape(1, dv))


def _nsa_layer(x, g_norm, sc, sh, gate, w_in, w_out, q_g, k_g, cmp_pos, cmp_w1, cmp_w2,
               cos_t, sin_t, B, T):
    G, R, dh = NSA_KV_HEADS, NSA_GROUP, HEAD_DIM
    H = G * R
    qd, gd = H * dh, G * dh
    qscale = dh ** -0.5 * LOG2E
    cut = lambda n: w_in[:, qd + n * gd: qd + (n + 1) * gd]
    w_main = jnp.concatenate([w_in[:, :qd], cut(2), cut(4), cut(0), cut(1), cut(3), cut(5)],
                             axis=1).astype(BF16)
    w_gate = jnp.pad(w_in[:, qd + 6 * gd:], ((0, 0), (0, LANES - 3 * H))).astype(BF16)
    colgain = jnp.concatenate([jnp.tile(q_g * qscale, H), jnp.tile(k_g[1], G), jnp.tile(k_g[2], G),
                               jnp.ones((4 * gd,), F32)])
    proj = _mm1(x, g_norm, sc, sh, w_main, epilogue="rope", out_dtype=BF16, seq=T,
                rope=(colgain, cos_t, sin_t), n_rope=(qd + 2 * gd) // 1024)
    gates = _mm1(x, g_norm, sc, sh, w_gate, epilogue="sigmoid", out_dtype=F32, seq=T, tn=LANES)
    proj = proj.reshape(B, T, -1)

    nc = T // CMP_STRIDE
    kcvc = proj[:, :, qd + 2 * gd: qd + 4 * gd].reshape(B, nc, CMP_STRIDE, 2, G, dh)
    chunks = kcvc.transpose(3, 0, 4, 1, 2, 5).reshape(2, B * G * nc, CMP_STRIDE * dh)
    half = CMP_BLOCK // 2
    w1cat = jnp.concatenate([cmp_w1[:, :half].reshape(2, half * dh, CMP_HIDDEN),
                             cmp_w1[:, half:].reshape(2, half * dh, CMP_HIDDEN)], axis=2).astype(BF16)
    pe2 = jnp.stack([cmp_pos[:, :half].reshape(2, half * dh),
                     cmp_pos[:, half:].reshape(2, half * dh)], axis=1)
    pe2 = jnp.pad(pe2, ((0, 0), (0, 6), (0, 0))).astype(BF16)
    last = np.minimum(CMP_STRIDE * np.arange(nc) + CMP_BLOCK - 1, T - 1)
    tab = lambda t: jnp.broadcast_to(t.reshape(B, 1, T, dh)[:, :, last], (B, G, nc, dh)).reshape(-1, dh)
    cmp_kv = _compress(chunks, w1cat, pe2, cmp_w2.astype(BF16), k_g[0], tab(cos_t), tab(sin_t))
    cmp_kv = cmp_kv.reshape(2, B * G, nc, dh)

    gates_t = gates[:, :3 * H].reshape(B, T, 3, G, R).transpose(0, 3, 2, 4, 1).reshape(B, G, 3 * R, T)
    gates_t = jnp.pad(gates_t, ((0, 0), (0, 0), (0, 16 - 3 * R), (0, 0)))
    o = _nsa_attention(proj, cmp_kv, gates_t, B, T)
    return _mm2(o.reshape(B * T, qd), w_out.astype(BF16), x, gate, seq=T)


def _diff_layer(x, g_norm, sc, sh, gate, w_in, w_out, q_g, k_g, lam_p, sub_g, lam_init,
                cos_t, sin_t, B, T):
    dh = HEAD_DIM
    D = w_out.shape[0]
    qscale = dh ** -0.5 * LOG2E
    nh = D // dh
    colgain = jnp.concatenate([jnp.tile(q_g * qscale, nh), jnp.tile(k_g, nh), jnp.ones((D,), F32)])
    proj = _mm1(x, g_norm, sc, sh, w_in.astype(BF16), epilogue="rope", out_dtype=BF16, seq=T,
                rope=(colgain, cos_t, sin_t), n_rope=2 * D // 1024)
    o = _diff_attention(proj.reshape(B, T, -1), lam_p, sub_g, lam_init, B, T)
    return _mm2(o.reshape(B * T, D), w_out.astype(BF16), x, gate, seq=T)


def kernel(x, c, positions, ada_w, ada_b, attn_norm_g, mlp_norm_g, mlp_w1, mlp_w2, nsa_w_in, nsa_w_out, nsa_q_norm, nsa_k_norm, nsa_cmp_pos, nsa_cmp_w1, nsa_cmp_w2, diff_w_in, diff_w_out, diff_q_norm, diff_k_norm, diff_lambda, diff_sub_norm):
    B, T, D = x.shape
    depth = ada_w.shape[0]
    mod_all = _adaln(c, ada_w, ada_b)
    cos_t, sin_t = _rope_tables(positions)
    xf = x.reshape(B * T, D)
    for i in range(depth):
        sh1, sc1, g1, sh2, sc2, g2 = [mod_all[i, :, n * D:(n + 1) * D] for n in range(6)]
        j = i // 2
        if i % 2 == 0:
            xf = _nsa_layer(xf, attn_norm_g[i], sc1, sh1, g1, nsa_w_in[j], nsa_w_out[j],
                            nsa_q_norm[j], nsa_k_norm[j], nsa_cmp_pos[j], nsa_cmp_w1[j],
                            nsa_cmp_w2[j], cos_t, sin_t, B, T)
        else:
            lam_init = 0.8 - 0.6 * math.exp(-0.3 * i)
            xf = _diff_layer(xf, attn_norm_g[i], sc1, sh1, g1, diff_w_in[j], diff_w_out[j],
                             diff_q_norm[j], diff_k_norm[j], diff_lambda[j], diff_sub_norm[j],
                             lam_init, cos_t, sin_t, B, T)
        hid = _mm1(xf, mlp_norm_g[i], sc2, sh2, mlp_w1[i].astype(BF16), epilogue="sqrelu",
                   out_dtype=BF16, seq=T)
        xf = _mm2(hid, mlp_w2[i].astype(BF16), xf, g2, seq=T)
    return xf.reshape(B, T, D)
```

```python
import functools
import math

import numpy as np
import jax
import jax.numpy as jnp
from jax import lax
from jax.experimental import pallas as pl
from jax.experimental.pallas import tpu as pltpu

F32 = jnp.float32
BF16 = jnp.bfloat16

HEAD_DIM = 128
ROPE_THETA = 10000.0
NORM_EPS = 1e-6
NEG = -1e30
BIG = 1e30
LOG2E = math.log2(math.e)

NSA_KV_HEADS = 4
NSA_GROUP = 4
CMP_BLOCK = 32
CMP_STRIDE = 16
CMP_HIDDEN = 2 * HEAD_DIM
SLC_BLOCK = 64
SLC_TOPN = 16
WIN = 512

LANES = 128
V7X_VMEM_BYTES = 64 * 1024 * 1024
VMEM_CAP = V7X_VMEM_BYTES - 8 * 1024 * 1024

NT_DIMS = (((1,), (1,)), ((), ()))
TN_DIMS = (((0,), (0,)), ((), ()))


def _vmem_limit(nbytes):
    return int(min(VMEM_CAP, max(32 * 1024 * 1024, nbytes)))


def _adaln_body(c_ref, w_ref, b_ref, o_ref):
    c = c_ref[...]
    sc = (c * jax.nn.sigmoid(c)).astype(BF16)
    w = w_ref[0].astype(BF16)
    o_ref[0] = jnp.dot(sc, w, preferred_element_type=F32) + b_ref[0]


def _adaln(c, ada_w, ada_b, tn=1024):
    L, D, E = ada_w.shape
    B = c.shape[0]
    return pl.pallas_call(
        _adaln_body,
        out_shape=jax.ShapeDtypeStruct((L, B, E), F32),
        grid=(L, E // tn),
        in_specs=[
            pl.BlockSpec((B, D), lambda l, j: (0, 0)),
            pl.BlockSpec((1, D, tn), lambda l, j: (l, 0, j)),
            pl.BlockSpec((1, 1, tn), lambda l, j: (l, 0, j)),
        ],
        out_specs=pl.BlockSpec((1, B, tn), lambda l, j: (l, 0, j)),
        compiler_params=pltpu.CompilerParams(
            dimension_semantics=("parallel", "parallel"),
            vmem_limit_bytes=_vmem_limit(2 * D * tn * 4 + D * tn * 2 + (8 << 20))),
        name="adaln_mod",
    )(c, ada_w, ada_b.reshape(L, 1, E))


def _rope_tab_body(pos_ref, inv_ref, sgn_ref, cos_ref, sin_ref):
    ang = pos_ref[...] * inv_ref[...]
    cos_ref[...] = jnp.cos(ang)
    sin_ref[...] = jnp.sin(ang) * sgn_ref[...]


def _rope_tables(positions, tr=1024):
    n = positions.size
    half = HEAD_DIM // 2
    inv = ROPE_THETA ** (-jnp.arange(half, dtype=F32) / half)
    inv = jnp.concatenate([inv, inv]).reshape(1, HEAD_DIM)
    sgn = jnp.concatenate([-jnp.ones((half,), F32), jnp.ones((half,), F32)]).reshape(1, HEAD_DIM)
    posb = jnp.broadcast_to(positions.astype(F32).reshape(n, 1), (n, HEAD_DIM))
    row = pl.BlockSpec((tr, HEAD_DIM), lambda i: (i, 0))
    vec = pl.BlockSpec((1, HEAD_DIM), lambda i: (0, 0))
    return pl.pallas_call(
        _rope_tab_body,
        out_shape=(jax.ShapeDtypeStruct((n, HEAD_DIM), F32),) * 2,
        grid=(n // tr,),
        in_specs=[row, vec, vec],
        out_specs=(row, row),
        compiler_params=pltpu.CompilerParams(dimension_semantics=("parallel",)),
        name="rope_tables",
    )(posb, inv, sgn)


def _norm_rope(y, gain, cosv, sinv):
    ms = jnp.mean(y * y, axis=-1, keepdims=True)
    y = y * lax.rsqrt(ms + NORM_EPS) * gain
    return y * cosv + pltpu.roll(y, HEAD_DIM // 2, axis=1) * sinv


def _mm1_body(x_ref, g_ref, sc_ref, sh_ref, w_ref, *refs, epilogue, n_rope, rc, gated):
    wg_ref = og_ref = None
    if epilogue == "rope" and gated:
        cg_ref, cos_ref, sin_ref, wg_ref, o_ref, og_ref, h_scr = refs
    elif epilogue == "rope":
        cg_ref, cos_ref, sin_ref, o_ref, h_scr = refs
    else:
        o_ref, h_scr = refs
    j = pl.program_id(1)
    tm = x_ref.shape[0]
    tn = w_ref.shape[1]
    chunks = [slice(r * rc, (r + 1) * rc) for r in range(tm // rc)]

    @pl.when(j == 0)
    def _():
        for rows in chunks:
            x = x_ref[rows, :]
            ms = jnp.mean(x * x, axis=-1, keepdims=True)
            y = x * lax.rsqrt(ms + NORM_EPS) * g_ref[...]
            h = (y * (1.0 + sc_ref[0]) + sh_ref[0]).astype(BF16)
            h_scr[rows, :] = h
            if gated:
                og_ref[rows, :] = jax.nn.sigmoid(
                    jnp.dot(h, wg_ref[...], preferred_element_type=F32))

    def compute(rope):
        for rows in chunks:
            acc = jnp.dot(h_scr[rows, :], w_ref[...], preferred_element_type=F32)
            if epilogue == "sqrelu":
                r = jnp.maximum(acc, 0.0)
                o_ref[rows, :] = (r * r).astype(o_ref.dtype)
            elif rope:
                cosv = cos_ref[rows, :]
                sinv = sin_ref[rows, :]
                for h in range(tn // HEAD_DIM):
                    cols = slice(h * HEAD_DIM, (h + 1) * HEAD_DIM)
                    y = _norm_rope(acc[:, cols], cg_ref[:, cols], cosv, sinv)
                    o_ref[rows, cols] = y.astype(o_ref.dtype)
            else:
                o_ref[rows, :] = acc.astype(o_ref.dtype)

    if epilogue == "rope":
        pl.when(j < n_rope)(functools.partial(compute, True))
        pl.when(j >= n_rope)(functools.partial(compute, False))
    else:
        compute(False)


def _mm1(x, g, sc, sh, w, *, epilogue, out_dtype, seq, rope=None, n_rope=0, gate_w=None,
         tm=1024, tn=1024, rc=256):
    N, D = x.shape
    Nout = w.shape[1]
    B = sc.shape[0]
    gated = gate_w is not None
    bpt = seq // tm
    in_specs = [
        pl.BlockSpec((tm, D), lambda i, j: (i, 0)),
        pl.BlockSpec((1, D), lambda i, j: (0, 0)),
        pl.BlockSpec((1, 1, D), lambda i, j: (i // bpt, 0, 0)),
        pl.BlockSpec((1, 1, D), lambda i, j: (i // bpt, 0, 0)),
        pl.BlockSpec((D, tn), lambda i, j: (0, j)),
    ]
    args = [x, g.reshape(1, D), sc.reshape(B, 1, D), sh.reshape(B, 1, D), w]
    if epilogue == "rope":
        colgain, cos_t, sin_t = rope
        in_specs += [
            pl.BlockSpec((1, tn), lambda i, j: (0, j)),
            pl.BlockSpec((tm, HEAD_DIM), lambda i, j: (i, 0)),
            pl.BlockSpec((tm, HEAD_DIM), lambda i, j: (i, 0)),
        ]
        args += [colgain.reshape(1, Nout), cos_t, sin_t]
    out_shape = jax.ShapeDtypeStruct((N, Nout), out_dtype)
    out_specs = pl.BlockSpec((tm, tn), lambda i, j: (i, j))
    if gated:
        in_specs.append(pl.BlockSpec((D, LANES), lambda i, j: (0, 0)))
        args.append(gate_w)
        out_shape = (out_shape, jax.ShapeDtypeStruct((N, LANES), F32))
        out_specs = (out_specs, pl.BlockSpec((tm, LANES), lambda i, j: (i, 0)))
    osz = jnp.dtype(out_dtype).itemsize
    vmem = (2 * tm * D * 4 + 2 * D * tn * 2 + 2 * tm * tn * osz + tm * D * 2
            + 6 * tm * HEAD_DIM * 4 + 4 * rc * tn * 4 + (6 << 20))
    return pl.pallas_call(
        functools.partial(_mm1_body, epilogue=epilogue, n_rope=n_rope, rc=rc, gated=gated),
        out_shape=out_shape,
        grid=(N // tm, Nout // tn),
        in_specs=in_specs,
        out_specs=out_specs,
        scratch_shapes=[pltpu.VMEM((tm, D), BF16)],
        compiler_params=pltpu.CompilerParams(
            dimension_semantics=("parallel", "arbitrary"),
            vmem_limit_bytes=_vmem_limit(vmem)),
        name="mm1_" + epilogue,
    )(*args)


def _mm2_body(a_ref, w_ref, x_ref, gate_ref, o_ref, *, rc):
    for r in range(a_ref.shape[0] // rc):
        rows = slice(r * rc, (r + 1) * rc)
        acc = jnp.dot(a_ref[rows, :], w_ref[...], preferred_element_type=F32)
        o_ref[rows, :] = x_ref[rows, :] + gate_ref[0] * acc


def _mm2k_body(a_ref, w_ref, x_ref, gate_ref, o_ref, acc_ref, *, nk):
    k = pl.program_id(2)
    part = lambda: jnp.dot(a_ref[...], w_ref[...], preferred_element_type=F32)

    @pl.when(k == 0)
    def _():
        acc_ref[...] = part()

    @pl.when((k > 0) & (k < nk - 1))
    def _():
        acc_ref[...] += part()

    @pl.when(k == nk - 1)
    def _():
        o_ref[...] = x_ref[...] + gate_ref[0] * (acc_ref[...] + part())


def _mm2k(a, w, x, gate, *, seq, tm=1024, tn=1024, tk=2048):
    N, K = a.shape
    D = w.shape[1]
    B = gate.shape[0]
    bpt = seq // tm
    nk = K // tk
    assert nk >= 2
    vmem = 2 * tm * tk * 2 + 2 * tk * tn * 2 + 6 * tm * tn * 4 + (6 << 20)
    return pl.pallas_call(
        functools.partial(_mm2k_body, nk=nk),
        out_shape=jax.ShapeDtypeStruct((N, D), F32),
        grid=(N // tm, D // tn, nk),
        in_specs=[
            pl.BlockSpec((tm, tk), lambda i, j, k: (i, k)),
            pl.BlockSpec((tk, tn), lambda i, j, k: (k, j)),
            pl.BlockSpec((tm, tn), lambda i, j, k: (i, j)),
            pl.BlockSpec((1, 1, tn), lambda i, j, k: (i // bpt, 0, j)),
        ],
        out_specs=pl.BlockSpec((tm, tn), lambda i, j, k: (i, j)),
        scratch_shapes=[pltpu.VMEM((tm, tn), F32)],
        compiler_params=pltpu.CompilerParams(
            dimension_semantics=("parallel", "parallel", "arbitrary"),
            vmem_limit_bytes=_vmem_limit(vmem)),
        name="mm2k_residual",
    )(a, w, x, gate.reshape(B, 1, D))


def _mm2(a, w, x, gate, *, seq, tm, tn, rc=256):
    N, K = a.shape
    D = w.shape[1]
    B = gate.shape[0]
    bpt = seq // tm
    vmem = 2 * tm * K * 2 + 2 * K * tn * 2 + 4 * tm * tn * 4 + 2 * rc * tn * 4 + (6 << 20)
    return pl.pallas_call(
        functools.partial(_mm2_body, rc=rc),
        out_shape=jax.ShapeDtypeStruct((N, D), F32),
        grid=(N // tm, D // tn),
        in_specs=[
            pl.BlockSpec((tm, K), lambda i, j: (i, 0)),
            pl.BlockSpec((K, tn), lambda i, j: (0, j)),
            pl.BlockSpec((tm, tn), lambda i, j: (i, j)),
            pl.BlockSpec((1, 1, tn), lambda i, j: (i // bpt, 0, j)),
        ],
        out_specs=pl.BlockSpec((tm, tn), lambda i, j: (i, j)),
        compiler_params=pltpu.CompilerParams(
            dimension_semantics=("parallel", "parallel"),
            vmem_limit_bytes=_vmem_limit(vmem)),
        name="mm2_residual",
    )(a, w, x, gate.reshape(B, 1, D))


def _compress_body(x_ref, w1_ref, pe_ref, w2_ref, kg_ref, cos_ref, sin_ref, o_ref):
    kv = pl.program_id(0)
    tm = x_ref.shape[1]
    w1 = w1_ref[0]
    ab = jnp.dot(x_ref[0], w1, preferred_element_type=F32)
    peab = jnp.dot(pe_ref[0], w1, preferred_element_type=F32)
    pe_term = peab[0:1, :CMP_HIDDEN] + peab[1:2, CMP_HIDDEN:]
    nxt = pltpu.roll(ab[:, CMP_HIDDEN:], tm - 1, axis=0)
    pre = ab[:, :CMP_HIDDEN] + nxt + pe_term
    hid = pre * jax.nn.sigmoid(pre)
    cmp = jnp.dot(hid.astype(BF16), w2_ref[0], preferred_element_type=F32)

    @pl.when(kv == 0)
    def _():
        o_ref[0] = _norm_rope(cmp, kg_ref[...], cos_ref[...], sin_ref[...]).astype(BF16)

    @pl.when(kv == 1)
    def _():
        o_ref[0] = cmp.astype(BF16)


def _compress(chunks, w1cat, pe2, w2, kg, cos_c, sin_c, tm=1024):
    _, M, KC = chunks.shape
    H2 = w1cat.shape[2]
    tm = min(tm, M)
    return pl.pallas_call(
        _compress_body,
        out_shape=jax.ShapeDtypeStruct((2, M, HEAD_DIM), BF16),
        grid=(2, M // tm),
        in_specs=[
            pl.BlockSpec((1, tm, KC), lambda kv, i: (kv, i, 0)),
            pl.BlockSpec((1, KC, H2), lambda kv, i: (kv, 0, 0)),
            pl.BlockSpec((1, 8, KC), lambda kv, i: (kv, 0, 0)),
            pl.BlockSpec((1, CMP_HIDDEN, HEAD_DIM), lambda kv, i: (kv, 0, 0)),
            pl.BlockSpec((1, HEAD_DIM), lambda kv, i: (0, 0)),
            pl.BlockSpec((tm, HEAD_DIM), lambda kv, i: (i, 0)),
            pl.BlockSpec((tm, HEAD_DIM), lambda kv, i: (i, 0)),
        ],
        out_specs=pl.BlockSpec((1, tm, HEAD_DIM), lambda kv, i: (kv, i, 0)),
        compiler_params=pltpu.CompilerParams(
            dimension_semantics=("parallel", "parallel"),
            vmem_limit_bytes=_vmem_limit(2 * tm * KC * 2 + 2 * KC * H2 * 2 + (16 << 20))),
        name="nsa_compress",
    )(chunks, w1cat, pe2, w2, kg.reshape(1, HEAD_DIM), cos_c, sin_c)


def _online_update(ss, v, stats, accs):
    out = []
    for s, (m, l), acc in zip(ss, stats, accs):
        m_new = jnp.maximum(m, jnp.max(s, axis=0, keepdims=True))
        alpha = jnp.exp2(m - m_new)
        p = jnp.exp2(s - m_new)
        l_new = alpha * l + jnp.sum(p, axis=0, keepdims=True)
        pv = lax.dot_general(v, p.astype(BF16), TN_DIMS, preferred_element_type=F32)
        acc[...] = alpha * acc[...] + pv
        out.append((m_new, l_new))
    return out


def _causal_sweep(qk_fn, v_fn, accs, n_tile, tq):
    for acc in accs:
        acc[...] = jnp.zeros_like(acc)
    stats = [(jnp.full((1, tq), NEG, F32), jnp.zeros((1, tq), F32))] * len(accs)
    ss = qk_fn(0)
    for kt in range(n_tile + 1):
        ss_next = qk_fn((kt + 1) * tq) if kt < n_tile else None
        if kt == n_tile:
            causal = _tri_bias(tq, True)
            ss = [s + causal for s in ss]
        stats = _online_update(ss, v_fn(kt * tq), stats, accs)
        ss = ss_next
    return [acc[...] / l for acc, (_, l) in zip(accs, stats)]


def _tri_bias(n, lower):
    row_i = lax.broadcasted_iota(jnp.int32, (n, n), 0)
    col_i = lax.broadcasted_iota(jnp.int32, (n, n), 1)
    return jnp.where(row_i <= col_i if lower else row_i > col_i, 0.0, NEG)


def _window_branch(q_ref, kw_ref, vw_ref, h, t0, tq):
    hw = WIN // 2
    upper, causal = _tri_bias(hw, False), _tri_bias(hw, True)
    outs = []
    for q0 in range(t0, t0 + tq, hw):
        q = q_ref[0, q0 - t0:q0 - t0 + hw, h * HEAD_DIM:(h + 1) * HEAD_DIM]
        tiles = [(k0, b) for k0, b in ((q0 - 2 * hw, upper), (q0 - hw, None), (q0, causal)) if k0 >= 0]
        ss = []
        for k0, bias in tiles:
            s = lax.dot_general(kw_ref[0, k0:k0 + hw, :], q, NT_DIMS, preferred_element_type=F32)
            ss.append(s if bias is None else s + bias)
        m = functools.reduce(jnp.maximum, [jnp.max(s, axis=0, keepdims=True) for s in ss])
        ps = [jnp.exp2(s - m) for s in ss]
        l = functools.reduce(jnp.add, [jnp.sum(p, axis=0, keepdims=True) for p in ps])
        p = jnp.concatenate([p.astype(BF16) for p in ps], axis=0)
        v = vw_ref[0, tiles[0][0]:q0 + hw, :]
        outs.append(lax.dot_general(v, p, TN_DIMS, preferred_element_type=F32) / l)
    return jnp.concatenate(outs, axis=1)


def _nsa_tile(n_tile, q_ref, ks_ref, kw_ref, vs_ref, vw_ref, kc_ref, vc_ref, gt_ref, et_ref, mt_ref,
              o_ref, qa_scr, acc_s, *, tq, n_sel, n_slc):
    t0 = n_tile * tq
    R = NSA_GROUP
    nc = kc_ref.shape[2]
    qh = lambda h: q_ref[0, :, h * HEAD_DIM:(h + 1) * HEAD_DIM]
    o_win = [_window_branch(q_ref, kw_ref, vw_ref, h, t0, tq) for h in range(R)]

    kc = kc_ref[0, 0]
    vc = vc_ref[0, 0]
    n_id = lax.broadcasted_iota(jnp.int32, (nc, tq), 0)
    t_id = t0 + lax.broadcasted_iota(jnp.int32, (nc, tq), 1)
    cm = n_id * CMP_STRIDE + (CMP_BLOCK - 1) <= t_id
    cbias = jnp.where(cm, 0.0, NEG)
    cmf = jnp.where(cm, 1.0, 0.0)
    o_cmp = []
    p_sum = jnp.zeros((nc, tq), F32)
    for h in range(R):
        s = lax.dot_general(kc, qh(h), NT_DIMS, preferred_element_type=F32) + cbias
        e = jnp.exp2(s - jnp.max(s, axis=0, keepdims=True)) * cmf
        p = e / jnp.maximum(jnp.sum(e, axis=0, keepdims=True), 1e-30)
        o_cmp.append(lax.dot_general(vc, p.astype(BF16), TN_DIMS, preferred_element_type=F32))
        p_sum = p_sum + p

    p_hi = p_sum.astype(BF16)
    p_lo = (p_sum - p_hi.astype(F32)).astype(BF16)
    mt = mt_ref[...]
    imp = (jnp.dot(mt, p_hi, preferred_element_type=F32)
           + jnp.dot(mt, p_lo, preferred_element_type=F32))
    j_id = lax.broadcasted_iota(jnp.int32, (n_slc, tq), 0)
    tb = jnp.right_shift(t0 + lax.broadcasted_iota(jnp.int32, (n_slc, tq), 1),
                         SLC_BLOCK.bit_length() - 1)
    imp = jnp.where(j_id <= tb, imp, NEG)
    imp = jnp.where(j_id == 0, BIG, jnp.where(j_id == tb, BIG, jnp.where(j_id == tb - 1, BIG, imp)))
    rank = jnp.zeros((n_slc, tq), F32)
    for jp in range(n_slc):
        row = imp[jp:jp + 1, :]
        gt = jnp.where(row > imp, 1.0, 0.0)
        ge = jnp.where(row >= imp, 1.0, 0.0)
        rank = rank + jnp.where(j_id > jp, ge, gt)
    selb = jnp.where(rank < n_sel, 0.0, NEG)
    selb = jnp.concatenate([selb, jnp.zeros((LANES - n_slc, tq), F32)], axis=0).T.astype(BF16)
    for h in range(R):
        qa_scr[h] = jnp.concatenate([qh(h), selb], axis=1)

    def slc_scores(k0):
        k_aug = jnp.concatenate([ks_ref[0, k0:k0 + tq, :], et_ref[k0:k0 + tq, :]], axis=1)
        return [lax.dot_general(k_aug, qa_scr[h], NT_DIMS, preferred_element_type=F32)
                for h in range(R)]

    o_slc = _causal_sweep(slc_scores, lambda k0: vs_ref[0, k0:k0 + tq, :],
                          [acc_s.at[h] for h in range(R)], n_tile, tq)

    gt = gt_ref[0, 0]
    for h in range(R):
        o = (gt[h:h + 1, :] * o_cmp[h] + gt[R + h:R + h + 1, :] * o_slc[h]
             + gt[2 * R + h:2 * R + h + 1, :] * o_win[h])
        o_ref[0, :, h * HEAD_DIM:(h + 1) * HEAD_DIM] = o.T.astype(o_ref.dtype)


def _nsa_body(*refs, tq, n_sel, n_slc):
    qi = pl.program_id(2)
    for n_tile in range(refs[1].shape[1] // tq):
        pl.when(qi == n_tile)(functools.partial(_nsa_tile, n_tile, *refs, tq=tq, n_sel=n_sel,
                                                n_slc=n_slc))


def _cmp_to_slc_map_t(n_cmp_pad, n_slc):
    cs = CMP_STRIDE * np.arange(n_cmp_pad)[None, :]
    ss = SLC_BLOCK * np.arange(n_slc)[:, None]
    ov = np.minimum(cs + CMP_BLOCK, ss + SLC_BLOCK) - np.maximum(cs, ss)
    return (np.clip(ov, 0, None) / CMP_STRIDE).astype(np.float32)


def _nsa_attention(proj, cmp_kv, gates_t, B, T):
    G, R = NSA_KV_HEADS, NSA_GROUP
    tq = WIN
    nc = T // CMP_STRIDE
    n_slc = T // SLC_BLOCK
    n_sel = min(SLC_TOPN, n_slc)
    qw = R * HEAD_DIM
    kb = (G * qw) // HEAD_DIM
    et = (np.arange(T)[:, None] // SLC_BLOCK == np.arange(LANES)[None, :]).astype(np.float32)
    mt = _cmp_to_slc_map_t(nc, n_slc)
    kv_spec = lambda off: pl.BlockSpec((1, T, HEAD_DIM), lambda b, g, i: (b, 0, off + g))
    return pl.pallas_call(
        functools.partial(_nsa_body, tq=tq, n_sel=n_sel, n_slc=n_slc),
        out_shape=jax.ShapeDtypeStruct((B, T, G * qw), BF16),
        grid=(B, G, T // tq),
        in_specs=[
            pl.BlockSpec((1, tq, qw), lambda b, g, i: (b, i, g)),
            kv_spec(kb), kv_spec(kb + G), kv_spec(kb + 4 * G), kv_spec(kb + 5 * G),
            pl.BlockSpec((1, 1, nc, HEAD_DIM), lambda b, g, i: (0, b * G + g, 0, 0)),
            pl.BlockSpec((1, 1, nc, HEAD_DIM), lambda b, g, i: (1, b * G + g, 0, 0)),
            pl.BlockSpec((1, 1, 16, tq), lambda b, g, i: (b, g, 0, i)),
            pl.BlockSpec((T, LANES), lambda b, g, i: (0, 0)),
            pl.BlockSpec((n_slc, nc), lambda b, g, i: (0, 0)),
        ],
        out_specs=pl.BlockSpec((1, tq, qw), lambda b, g, i: (b, i, g)),
        scratch_shapes=[pltpu.VMEM((R, tq, 2 * HEAD_DIM), BF16),
                        pltpu.VMEM((R, HEAD_DIM, tq), F32)],
        compiler_params=pltpu.CompilerParams(
            dimension_semantics=("parallel", "parallel", "arbitrary"),
            vmem_limit_bytes=_vmem_limit(48 << 20)),
        name="nsa_attention",
    )(proj, proj, proj, proj, proj, cmp_kv, cmp_kv, gates_t,
      jnp.asarray(et, BF16), jnp.asarray(mt, BF16))


def _diff_tile(n_tile, q_ref, k_ref, v_ref, lam_ref, sg_ref, o_ref, acc, *, tq, lam_init):
    dv = 2 * HEAD_DIM

    def scores(k0):
        return [lax.dot_general(k_ref[0, k0:k0 + tq, c * HEAD_DIM:(c + 1) * HEAD_DIM],
                                q_ref[0, :, c * HEAD_DIM:(c + 1) * HEAD_DIM],
                                NT_DIMS, preferred_element_type=F32) for c in range(2)]

    o1, o2 = _causal_sweep(scores, lambda k0: v_ref[0, k0:k0 + tq, :],
                           [acc.at[0], acc.at[1]], n_tile, tq)

    lp = lam_ref[...]
    lam = (jnp.exp(jnp.sum(lp[0:1] * lp[1:2], axis=-1, keepdims=True))
           - jnp.exp(jnp.sum(lp[2:3] * lp[3:4], axis=-1, keepdims=True)) + lam_init)
    o = o1 - lam * o2
    ms = jnp.mean(o * o, axis=0, keepdims=True)
    o = o * lax.rsqrt(ms + NORM_EPS)
    sg = sg_ref[...] * (1.0 - lam_init)
    for c in range(dv // HEAD_DIM):
        cols = slice(c * HEAD_DIM, (c + 1) * HEAD_DIM)
        o_ref[0, :, cols] = (o[cols, :].T * sg[:, cols]).astype(o_ref.dtype)


def _diff_body(*refs, tq, lam_init):
    qi = pl.program_id(2)
    for n_tile in range(refs[1].shape[1] // tq):
        pl.when(qi == n_tile)(functools.partial(_diff_tile, n_tile, *refs, tq=tq, lam_init=lam_init))


def _diff_attention(proj, lam_p, sub_g, lam_init, B, T, tq=512):
    dv = 2 * HEAD_DIM
    H = proj.shape[2] // (3 * dv)
    return pl.pallas_call(
        functools.partial(_diff_body, tq=tq, lam_init=lam_init),
        out_shape=jax.ShapeDtypeStruct((B, T, H * dv), BF16),
        grid=(B, H, T // tq),
        in_specs=[
            pl.BlockSpec((1, tq, dv), lambda b, h, i: (b, i, h)),
            pl.BlockSpec((1, T, dv), lambda b, h, i: (b, 0, H + h)),
            pl.BlockSpec((1, T, dv), lambda b, h, i: (b, 0, 2 * H + h)),
            pl.BlockSpec((4, HEAD_DIM), lambda b, h, i: (0, 0)),
            pl.BlockSpec((1, dv), lambda b, h, i: (0, 0)),
        ],
        out_specs=pl.BlockSpec((1, tq, dv), lambda b, h, i: (b, i, h)),
        scratch_shapes=[pltpu.VMEM((2, dv, tq), F32)],
        compiler_params=pltpu.CompilerParams(
            dimension_semantics=("parallel", "parallel", "arbitrary"),
            vmem_limit_bytes=_vmem_limit(48 << 20)),
        name="diff_attention",
    )(proj, proj, proj, lam_p, sub_g.reshape(1, dv))


def _nsa_layer(x, g_norm, sc, sh, gate, w_in, w_out, q_g, k_g, cmp_pos, cmp_w1, cmp_w2,
               cos_t, sin_t, B, T):
    G, R, dh = NSA_KV_HEADS, NSA_GROUP, HEAD_DIM
    H = G * R
    qd, gd = H * dh, G * dh
    qscale = dh ** -0.5 * LOG2E
    cut = lambda n: w_in[:, qd + n * gd: qd + (n + 1) * gd]
    w_main = jnp.concatenate([w_in[:, :qd], cut(2), cut(4), cut(0), cut(1), cut(3), cut(5)],
                             axis=1).astype(BF16)
    w_gate = jnp.pad(w_in[:, qd + 6 * gd:], ((0, 0), (0, LANES - 3 * H))).astype(BF16)
    colgain = jnp.concatenate([jnp.tile(q_g * qscale, H), jnp.tile(k_g[1], G), jnp.tile(k_g[2], G),
                               jnp.ones((4 * gd,), F32)])
    proj, gates = _mm1(x, g_norm, sc, sh, w_main, epilogue="rope", out_dtype=BF16, seq=T,
                       rope=(colgain, cos_t, sin_t), n_rope=(qd + 2 * gd) // 1024, gate_w=w_gate)
    proj = proj.reshape(B, T, -1)

    nc = T // CMP_STRIDE
    kcvc = proj[:, :, qd + 2 * gd: qd + 4 * gd].reshape(B, nc, CMP_STRIDE, 2, G, dh)
    chunks = kcvc.transpose(3, 0, 4, 1, 2, 5).reshape(2, B * G * nc, CMP_STRIDE * dh)
    half = CMP_BLOCK // 2
    w1cat = jnp.concatenate([cmp_w1[:, :half].reshape(2, half * dh, CMP_HIDDEN),
                             cmp_w1[:, half:].reshape(2, half * dh, CMP_HIDDEN)], axis=2).astype(BF16)
    pe2 = jnp.stack([cmp_pos[:, :half].reshape(2, half * dh),
                     cmp_pos[:, half:].reshape(2, half * dh)], axis=1)
    pe2 = jnp.pad(pe2, ((0, 0), (0, 6), (0, 0))).astype(BF16)
    last = np.minimum(CMP_STRIDE * np.arange(nc) + CMP_BLOCK - 1, T - 1)
    tab = lambda t: jnp.broadcast_to(t.reshape(B, 1, T, dh)[:, :, last], (B, G, nc, dh)).reshape(-1, dh)
    cmp_kv = _compress(chunks, w1cat, pe2, cmp_w2.astype(BF16), k_g[0], tab(cos_t), tab(sin_t))
    cmp_kv = cmp_kv.reshape(2, B * G, nc, dh)

    gates_t = gates[:, :3 * H].reshape(B, T, 3, G, R).transpose(0, 3, 2, 4, 1).reshape(B, G, 3 * R, T)
    gates_t = jnp.pad(gates_t, ((0, 0), (0, 0), (0, 16 - 3 * R), (0, 0)))
    o = _nsa_attention(proj, cmp_kv, gates_t, B, T)
    return _mm2(o.reshape(B * T, qd), w_out.astype(BF16), x, gate, seq=T, tm=1024, tn=1024)


def _diff_layer(x, g_norm, sc, sh, gate, w_in, w_out, q_g, k_g, lam_p, sub_g, lam_init,
                cos_t, sin_t, B, T):
    dh = HEAD_DIM
    D = w_out.shape[0]
    qscale = dh ** -0.5 * LOG2E
    nh = D // dh
    colgain = jnp.concatenate([jnp.tile(q_g * qscale, nh), jnp.tile(k_g, nh), jnp.ones((D,), F32)])
    proj = _mm1(x, g_norm, sc, sh, w_in.astype(BF16), epilogue="rope", out_dtype=BF16, seq=T,
                rope=(colgain, cos_t, sin_t), n_rope=2 * D // 1024)
    o = _diff_attention(proj.reshape(B, T, -1), lam_p, sub_g, lam_init, B, T)
    return _mm2(o.reshape(B * T, D), w_out.astype(BF16), x, gate, seq=T, tm=1024, tn=1024)


def kernel(x, c, positions, ada_w, ada_b, attn_norm_g, mlp_norm_g, mlp_w1, mlp_w2, nsa_w_in, nsa_w_out, nsa_q_norm, nsa_k_norm, nsa_cmp_pos, nsa_cmp_w1, nsa_cmp_w2, diff_w_in, diff_w_out, diff_q_norm, diff_k_norm, diff_lambda, diff_sub_norm):
    B, T, D = x.shape
    depth = ada_w.shape[0]
    mod_all = _adaln(c, ada_w, ada_b)
    cos_t, sin_t = _rope_tables(positions)
    xf = x.reshape(B * T, D)
    for i in range(depth):
        sh1, sc1, g1, sh2, sc2, g2 = [mod_all[i, :, n * D:(n + 1) * D] for n in range(6)]
        j = i // 2
        if i % 2 == 0:
            xf = _nsa_layer(xf, attn_norm_g[i], sc1, sh1, g1, nsa_w_in[j], nsa_w_out[j],
                            nsa_q_norm[j], nsa_k_norm[j], nsa_cmp_pos[j], nsa_cmp_w1[j],
                            nsa_cmp_w2[j], cos_t, sin_t, B, T)
        else:
            lam_init = 0.8 - 0.6 * math.exp(-0.3 * i)
            xf = _diff_layer(xf, attn_norm_g[i], sc1, sh1, g1, diff_w_in[j], diff_w_out[j],
                             diff_q_norm[j], diff_k_norm[j], diff_lambda[j], diff_sub_norm[j],
                             lam_init, cos_t, sin_t, B, T)
        hid = _mm1(xf, mlp_norm_g[i], sc2, sh2, mlp_w1[i].astype(BF16), epilogue="sqrelu",
                   out_dtype=BF16, seq=T)
        xf = _mm2k(hid, mlp_w2[i].astype(BF16), xf, g2, seq=T)
    return xf.reshape(B, T, D)
```
